```python
import math
import jax, jax.numpy as jnp
from jax import lax
import numpy as np

D_MODEL = 1024
BATCH = 8
SEQ = 4096
DEPTH = 2

HEAD_DIM = 64
N_MOBA_HEADS = D_MODEL // (2 * HEAD_DIM)
N_DSA_HEADS = D_MODEL // (2 * HEAD_DIM)
MOBA_WIDTH = N_MOBA_HEADS * HEAD_DIM
DSA_WIDTH = N_DSA_HEADS * HEAD_DIM
MIX_WIDTH = MOBA_WIDTH + DSA_WIDTH
ROPE_THETA = 500000.0
ROPE_DIM = HEAD_DIM // 4
MOBA_BLOCK = 256
MOBA_TOPK = 3
MOBA_Q_CHUNK = 32
N_IDX_HEADS = 8
IDX_DIM = 64
DSA_TOPK = 256
DSA_Q_CHUNK = 64
D_FF = 2816
CONV_WIDTH = 3
NORM_EPS = 1e-6
N_MOD = 6
PROJ_WIDTH = 3 * MOBA_WIDTH + 3 * DSA_WIDTH + N_IDX_HEADS * IDX_DIM + IDX_DIM + N_IDX_HEADS

kernel_name = "moba_dsa_hymba_hybrid_block"


def rms_norm(x, g):
    xf = x.astype(jnp.float32)
    y = xf * lax.rsqrt(jnp.mean(xf * xf, axis=-1, keepdims=True) + NORM_EPS)
    return (y * g.astype(jnp.float32)).astype(x.dtype)


def rope_tables(seq):
    pos = jnp.arange(seq, dtype=jnp.float32)
    inv_freq = ROPE_THETA ** (-jnp.arange(0, ROPE_DIM, 2, dtype=jnp.float32) / ROPE_DIM)
    ang = pos[:, None] * inv_freq[None, :]
    return jnp.cos(ang), jnp.sin(ang)


def partial_rope(x, cos, sin):
    xf = x.astype(jnp.float32)
    half = ROPE_DIM // 2
    x1, x2, xp = xf[..., :half], xf[..., half:ROPE_DIM], xf[..., ROPE_DIM:]
    c = cos[None, :, None, :]
    s = sin[None, :, None, :]
    out = jnp.concatenate([x1 * c - x2 * s, x2 * c + x1 * s, xp], axis=-1)
    return out.astype(x.dtype)


def moba_attention(q, k, v):
    B, S, H, Dh = q.shape
    nb = -(-S // MOBA_BLOCK)
    pad = nb * MOBA_BLOCK - S
    kp = jnp.pad(k, ((0, 0), (0, pad), (0, 0), (0, 0)))
    vp = jnp.pad(v, ((0, 0), (0, pad), (0, 0), (0, 0)))
    kb = kp.reshape(B, nb, MOBA_BLOCK, H, Dh)
    vb = vp.reshape(B, nb, MOBA_BLOCK, H, Dh)
    k_mean = jnp.mean(kb.astype(jnp.float32), axis=2)
    kb_g = kb.transpose(0, 3, 1, 2, 4).reshape(B, H, nb, MOBA_BLOCK * Dh)
    vb_g = vb.transpose(0, 3, 1, 2, 4).reshape(B, H, nb, MOBA_BLOCK * Dh)
    n_sel = min(MOBA_TOPK, nb)
    C = MOBA_Q_CHUNK
    n_chunks = S // C
    scale = HEAD_DIM ** -0.5
    qc = q.reshape(B, n_chunks, C, H, Dh).swapaxes(0, 1)
    blk_ids = jnp.arange(nb)

    def chunk(args):
        ci, qb = args
        t0 = ci * C
        cur = t0 // MOBA_BLOCK
        tq = t0 + jnp.arange(C)
        qf = qb.astype(jnp.float32)
        gate = jnp.einsum('bchd,bnhd->bhcn', qf, k_mean)
        gate = jnp.where(blk_ids < cur, gate, -jnp.inf)
        _, sel = lax.top_k(gate, n_sel)
        sel_ok = jnp.repeat(sel < cur, MOBA_BLOCK, axis=-1)
        gidx = sel.reshape(B, H, C * n_sel)[..., None]
        kg = jnp.take_along_axis(kb_g, gidx, axis=2).reshape(B, H, C, n_sel * MOBA_BLOCK, Dh)
        vg = jnp.take_along_axis(vb_g, gidx, axis=2).reshape(B, H, C, n_sel * MOBA_BLOCK, Dh)
        k_own = lax.dynamic_slice_in_dim(kp, cur * MOBA_BLOCK, MOBA_BLOCK, axis=1)
        v_own = lax.dynamic_slice_in_dim(vp, cur * MOBA_BLOCK, MOBA_BLOCK, axis=1)
        own_pos = cur * MOBA_BLOCK + jnp.arange(MOBA_BLOCK)
        causal = own_pos[None, :] <= tq[:, None]
        s_sel = jnp.einsum('bchd,bhckd->bhck', qb, kg).astype(jnp.float32) * scale
        s_own = jnp.einsum('bchd,bkhd->bhck', qb, k_own).astype(jnp.float32) * scale
        s_sel = jnp.where(sel_ok, s_sel, -jnp.inf)
        s_own = jnp.where(causal[None, None], s_own, -jnp.inf)
        p = jax.nn.softmax(jnp.concatenate([s_sel, s_own], axis=-1), axis=-1).astype(v.dtype)
        p_sel, p_own = p[..., : n_sel * MOBA_BLOCK], p[..., n_sel * MOBA_BLOCK:]
        return (jnp.einsum('bhck,bhckd->bchd', p_sel, vg)
                + jnp.einsum('bhck,bkhd->bchd', p_own, v_own))

    out = lax.map(chunk, (jnp.arange(n_chunks), qc))
    return out.swapaxes(0, 1).reshape(B, S, H * Dh)


def dsa_attention(q, k, v, q_idx, k_idx, w_idx):
    B, S, H, Dh = q.shape
    n_keep = min(DSA_TOPK, S // 4)
    C = DSA_Q_CHUNK
    n_chunks = S // C
    scale = HEAD_DIM ** -0.5
    idx_scale = IDX_DIM ** -0.5
    w_scale = N_IDX_HEADS ** -0.5
    kf = k.reshape(B, S, H * Dh)
    vf = v.reshape(B, S, H * Dh)
    kif = k_idx.astype(jnp.float32)
    key_pos = jnp.arange(S)
    qc = q.reshape(B, n_chunks, C, H, Dh).swapaxes(0, 1)
    qic = q_idx.reshape(B, n_chunks, C, N_IDX_HEADS, IDX_DIM).swapaxes(0, 1)
    wc = w_idx.reshape(B, n_chunks, C, N_IDX_HEADS).swapaxes(0, 1)

    def chunk(args):
        ci, qb, qib, wb = args
        tq = ci * C + jnp.arange(C)
        logits = jnp.einsum('bchd,bsd->bchs', qib.astype(jnp.float32), kif) * idx_scale
        score = jnp.einsum('bch,bchs->bcs', wb.astype(jnp.float32) * w_scale, jax.nn.relu(logits))
        admissible = key_pos[None, :] <= tq[:, None]
        score = jnp.where(admissible[None], score, -jnp.inf)
        _, sel = lax.top_k(score, n_keep)
        sel_ok = sel <= tq[None, :, None]
        gidx = sel.reshape(B, C * n_keep)[..., None]
        kg = jnp.take_along_axis(kf, gidx, axis=1).reshape(B, C, n_keep, H, Dh)
        vg = jnp.take_along_axis(vf, gidx, axis=1).reshape(B, C, n_keep, H, Dh)
        s = jnp.einsum('bchd,bckhd->bhck', qb, kg).astype(jnp.float32) * scale
        s = jnp.where(sel_ok[:, None], s, -jnp.inf)
        p = jax.nn.softmax(s, axis=-1).astype(v.dtype)
        return jnp.einsum('bhck,bckhd->bchd', p, vg)

    out = lax.map(chunk, (jnp.arange(n_chunks), qc, qic, wc))
    return out.swapaxes(0, 1).reshape(B, S, H * Dh)


def token_mixer(h, w_in, g_moba_out, g_dsa_out, w_out, cos, sin):
    B, S, _ = h.shape
    proj = h @ w_in
    sizes = [MOBA_WIDTH] * 3 + [DSA_WIDTH] * 3 + [N_IDX_HEADS * IDX_DIM, IDX_DIM, N_IDX_HEADS]
    offs = []
    acc = 0
    for sz in sizes[:-1]:
        acc += sz
        offs.append(acc)
    mq, mk, mv, dq, dk, dv, qi, ki, wi = jnp.split(proj, offs, axis=-1)
    mq = partial_rope(mq.reshape(B, S, N_MOBA_HEADS, HEAD_DIM), cos, sin)
    mk = partial_rope(mk.reshape(B, S, N_MOBA_HEADS, HEAD_DIM), cos, sin)
    mv = mv.reshape(B, S, N_MOBA_HEADS, HEAD_DIM)
    dq = partial_rope(dq.reshape(B, S, N_DSA_HEADS, HEAD_DIM), cos, sin)
    dk = partial_rope(dk.reshape(B, S, N_DSA_HEADS, HEAD_DIM), cos, sin)
    dv = dv.reshape(B, S, N_DSA_HEADS, HEAD_DIM)
    qi = partial_rope(qi.reshape(B, S, N_IDX_HEADS, IDX_DIM), cos, sin)
    ki = partial_rope(ki[:, :, None, :], cos, sin)[:, :, 0, :]
    o_moba = moba_attention(mq, mk, mv)
    o_dsa = dsa_attention(dq, dk, dv, qi, ki, wi)
    o = jnp.concatenate([rms_norm(o_moba, g_moba_out), rms_norm(o_dsa, g_dsa_out)], axis=-1)
    return o @ w_out


def causal_depthwise_conv(u, w, b):
    out = lax.conv_general_dilated(
        u, w[:, None, :].astype(u.dtype), window_strides=(1,),
        padding=[(CONV_WIDTH - 1, 0)], dimension_numbers=('NWC', 'WIO', 'NWC'),
        feature_group_count=u.shape[-1])
    return out + b.astype(u.dtype)


def conv_glu_ffn(h, w_up_act, w_up_lin, w_conv, b_conv, w_down):
    a = causal_depthwise_conv(h @ w_up_act, w_conv, b_conv)
    return (jax.nn.gelu(a) * (h @ w_up_lin)) @ w_down


def setup_inputs(seed: int = 0) -> dict:
    key = jax.random.key(seed)
    ks = jax.random.split(key, 20)
    f32 = jnp.float32
    nrm = lambda k, shape, s: jax.random.normal(k, shape, f32) * s
    return {
        "x": nrm(ks[0], (BATCH, SEQ, D_MODEL), 1.0),
        "c": nrm(ks[1], (BATCH, D_MODEL), 1.0),
        "w_ada": nrm(ks[2], (DEPTH, D_MODEL, N_MOD * D_MODEL), 0.5 * D_MODEL ** -0.5),
        "b_ada": nrm(ks[3], (DEPTH, N_MOD * D_MODEL), 0.02),
        "g_pre_mix": 1.0 + nrm(ks[4], (DEPTH, D_MODEL), 0.02),
        "w_in": nrm(ks[5], (DEPTH, D_MODEL, PROJ_WIDTH), D_MODEL ** -0.5),
        "g_moba_out": 1.0 + nrm(ks[6], (DEPTH, MOBA_WIDTH), 0.1),
        "g_dsa_out": 1.0 + nrm(ks[7], (DEPTH, DSA_WIDTH), 0.1),
        "w_out": nrm(ks[8], (DEPTH, MIX_WIDTH, D_MODEL), MIX_WIDTH ** -0.5),
        "g_post_mix": 1.0 + nrm(ks[9], (DEPTH, D_MODEL), 0.02),
        "g_pre_ffn": 1.0 + nrm(ks[10], (DEPTH, D_MODEL), 0.02),
        "w_up_act": nrm(ks[11], (DEPTH, D_MODEL, D_FF), D_MODEL ** -0.5),
        "w_up_lin": nrm(ks[12], (DEPTH, D_MODEL, D_FF), D_MODEL ** -0.5),
        "w_conv": nrm(ks[13], (DEPTH, CONV_WIDTH, D_FF), CONV_WIDTH ** -0.5),
        "b_conv": nrm(ks[14], (DEPTH, D_FF), 0.02),
        "w_down": nrm(ks[15], (DEPTH, D_FF, D_MODEL), D_FF ** -0.5),
        "g_post_ffn": 1.0 + nrm(ks[16], (DEPTH, D_MODEL), 0.02),
    }


def reference(x, c, w_ada, b_ada, g_pre_mix, w_in, g_moba_out, g_dsa_out, w_out, g_post_mix,
              g_pre_ffn, w_up_act, w_up_lin, w_conv, b_conv, w_down, g_post_ffn):
    cos, sin = rope_tables(x.shape[1])
    c_act = jax.nn.silu(c)
    for l in range(DEPTH):
        mod = c_act @ w_ada[l] + b_ada[l]
        sh_m, sc_m, gt_m, sh_f, sc_f, gt_f = jnp.split(mod[:, None, :], N_MOD, axis=-1)
        h = rms_norm(x, g_pre_mix[l]) * (1.0 + sc_m) + sh_m
        y = token_mixer(h, w_in[l], g_moba_out[l], g_dsa_out[l], w_out[l], cos, sin)
        x = x + gt_m * rms_norm(y, g_post_mix[l])
        h = rms_norm(x, g_pre_ffn[l]) * (1.0 + sc_f) + sh_f
        y = conv_glu_ffn(h, w_up_act[l], w_up_lin[l], w_conv[l], b_conv[l], w_down[l])
        x = x + gt_f * rms_norm(y, g_post_ffn[l])
    return x
```

```python
import functools

import jax
import jax.numpy as jnp
from jax import lax
from jax.experimental import pallas as pl
from jax.experimental.pallas import tpu as pltpu

F32 = jnp.float32
BF16 = jnp.bfloat16

D_MODEL = 1024
HEAD_DIM = 64
N_HEADS = 8
WIDTH = N_HEADS * HEAD_DIM
ROPE_THETA = 500000.0
ROPE_DIM = HEAD_DIM // 4
MOBA_BLOCK = 256
MOBA_TOPK = 3
N_IDX_HEADS = 8
IDX_DIM = 64
DSA_TOPK = 256
D_FF = 2816
CONV_WIDTH = 3
NORM_EPS = 1e-6
N_MOD = 6
PROJ_WIDTH = 6 * WIDTH + N_IDX_HEADS * IDX_DIM + IDX_DIM + N_IDX_HEADS

LANES = 128
SUBLANES = 8
PROJ_PAD = 29 * LANES
TAIL_COL = 7 * WIDTH
Q_SCALE = HEAD_DIM ** -0.5
W_SCALE = N_IDX_HEADS ** -0.5
NEG_INF = float("-inf")
M_INIT = -1e30
INT_MIN = -(2 ** 31)

TM_PROJ = 512
TM_FFN = 512
TQ = MOBA_BLOCK
TK = MOBA_BLOCK
FF_CHUNK = 256
VMEM_LIMIT = 52 * 1024 * 1024

_CONTRACT_LANES = (((1,), (1,)), ((), ()))


def _rms(x, g):
    return x * lax.rsqrt(jnp.mean(x * x, axis=-1, keepdims=True) + NORM_EPS) * g


def _dot_t(a, b):
    return lax.dot_general(a, b, _CONTRACT_LANES, preferred_element_type=F32)


def _resident(shape):
    return pl.BlockSpec(shape, lambda *_: (0,) * len(shape), pipeline_mode=pl.Buffered(1))


def _ada_kernel(c_ref, w_ref, b_ref, o_ref):
    c = c_ref[...]
    c_act = (c * jax.nn.sigmoid(c)).astype(BF16)
    o_ref[...] = jnp.dot(c_act, w_ref[...].astype(BF16), preferred_element_type=F32) + b_ref[...]


def _ada_call(c, w_ada, b_ada):
    depth, d, n = w_ada.shape
    b = c.shape[0]
    tn = n // 4
    return pl.pallas_call(
        _ada_kernel,
        grid=(depth, n // tn),
        in_specs=[
            pl.BlockSpec((b, d), lambda l, j: (0, 0)),
            pl.BlockSpec((None, d, tn), lambda l, j: (l, 0, j)),
            pl.BlockSpec((None, 1, tn), lambda l, j: (l, 0, j)),
        ],
        out_specs=pl.BlockSpec((None, b, tn), lambda l, j: (l, 0, j)),
        out_shape=jax.ShapeDtypeStruct((depth, b, n), F32),
        compiler_params=pltpu.CompilerParams(vmem_limit_bytes=VMEM_LIMIT),
        name="ada",
    )(c, w_ada, b_ada.reshape(depth, 1, n))


def _rope_tables(seq):
    pos = jnp.arange(seq, dtype=F32)
    inv_freq = ROPE_THETA ** (-jnp.arange(0, ROPE_DIM, 2, dtype=F32) / ROPE_DIM)
    ang = pos[:, None] * inv_freq[None, :]
    cos, sin = jnp.cos(ang), jnp.sin(ang)
    half = ROPE_DIM // 2
    d = jnp.arange(LANES) % HEAD_DIM
    f = d % half
    t_self = jnp.where(d < ROPE_DIM, cos[:, f], 1.0)
    t_up = jnp.where(d < half, -sin[:, f], 0.0)
    t_dn = jnp.where((d >= half) & (d < ROPE_DIM), sin[:, f], 0.0)
    return jnp.concatenate([t_self, t_up, t_dn], axis=1).astype(F32)


def _proj_kernel(x_ref, mod_ref, g_ref, w_ref, tab_ref,
                 mq_ref, mk_ref, mv_ref, dq_ref, dk_ref, dv_ref, qi_ref, kib_ref, kw_ref, km_ref):
    mod = mod_ref[...]
    h = _rms(x_ref[...], g_ref[...]) * (1.0 + mod[1:2]) + mod[0:1]
    hb = h.astype(BF16)
    tab = tab_ref[...]
    t_self, t_up, t_dn = tab[:, :LANES], tab[:, LANES:2 * LANES], tab[:, 2 * LANES:]
    half = ROPE_DIM // 2

    def rope(a):
        up = pltpu.roll(a, LANES - half, axis=1)
        dn = pltpu.roll(a, half, axis=1)
        return a * t_self + up * t_up + dn * t_dn

    def group(g_idx, roped, scaled):
        a = jnp.dot(hb, w_ref[:, g_idx * WIDTH:(g_idx + 1) * WIDTH], preferred_element_type=F32)
        if scaled:
            a = a * Q_SCALE
        if roped:
            a = jnp.concatenate([rope(a[:, c * LANES:(c + 1) * LANES]) for c in range(WIDTH // LANES)], axis=1)
        return a

    mq_ref[...] = group(0, True, True).astype(BF16)
    mk = group(1, True, False)
    mk_ref[...] = mk.astype(BF16)
    km_ref[...] = jnp.mean(mk.reshape(mk.shape[0] // MOBA_BLOCK, MOBA_BLOCK, WIDTH), axis=1)
    mv_ref[...] = group(2, False, False).astype(BF16)
    dq_ref[...] = group(3, True, True).astype(BF16)
    dk_ref[...] = group(4, True, False).astype(BF16)
    dv_ref[...] = group(5, False, False).astype(BF16)
    qi_ref[...] = group(6, True, True).astype(BF16)

    tail = jnp.dot(hb, w_ref[:, TAIL_COL:TAIL_COL + LANES], preferred_element_type=F32)
    lane = lax.broadcasted_iota(jnp.int32, tail.shape, 1)
    is_key = lane < IDX_DIM
    roped = rope(tail)
    kw_ref[...] = jnp.where(is_key, roped, tail)
    kib_ref[...] = jnp.where(is_key, roped, pltpu.roll(roped, IDX_DIM, axis=1)).astype(BF16)


def _proj_call(x2, mod, g_pre, w_in_b, tab, batch, seq):
    m = x2.shape[0]
    tm = TM_PROJ
    per_seq = seq // tm
    nblk = tm // MOBA_BLOCK
    row = lambda i: (i, 0)
    wide = lambda dt: jax.ShapeDtypeStruct((m, WIDTH), dt)
    return pl.pallas_call(
        _proj_kernel,
        grid=(m // tm,),
        in_specs=[
            pl.BlockSpec((tm, D_MODEL), row),
            pl.BlockSpec((None, SUBLANES, D_MODEL), lambda i: (i // per_seq, 0, 0)),
            _resident((1, D_MODEL)),
            _resident((D_MODEL, PROJ_PAD)),
            pl.BlockSpec((tm, 3 * LANES), lambda i: (i % per_seq, 0)),
        ],
        out_specs=[pl.BlockSpec((tm, WIDTH), row)] * 7 + [
            pl.BlockSpec((tm, LANES), row),
            pl.BlockSpec((tm, LANES), row),
            pl.BlockSpec((None, nblk, WIDTH), lambda i: (i, 0, 0)),
        ],
        out_shape=[wide(BF16)] * 7 + [
            jax.ShapeDtypeStruct((m, LANES), BF16),
            jax.ShapeDtypeStruct((m, LANES), F32),
            jax.ShapeDtypeStruct((m // tm, nblk, WIDTH), F32),
        ],
        compiler_params=pltpu.CompilerParams(
            dimension_semantics=("parallel",), vmem_limit_bytes=VMEM_LIMIT),
        name="proj",
    )(x2, mod, g_pre, w_in_b, tab)


def _softmax_step(s, m, l, acc, v_tile):
    m_new = jnp.maximum(m, jnp.max(s, axis=1, keepdims=True))
    alpha = jnp.exp(m - m_new)
    p = jnp.exp(s - m_new)
    l_new = alpha * l + jnp.sum(p, axis=1, keepdims=True)
    acc_new = alpha * acc + jnp.dot(p.astype(BF16), v_tile, preferred_element_type=F32)
    return m_new, l_new, acc_new


def _head_lane_mask(shape, hh):
    lane = lax.broadcasted_iota(jnp.int32, shape, 1)
    return (lane < HEAD_DIM) if hh == 0 else (lane >= HEAD_DIM)


def _causal_bias():
    row = lax.broadcasted_iota(jnp.int32, (TQ, TK), 0)
    col = lax.broadcasted_iota(jnp.int32, (TQ, TK), 1)
    return jnp.where(col <= row, 0.0, NEG_INF).astype(F32)


def _moba_kernel(q_ref, k_ref, v_ref, km_ref, g_ref, o_ref):
    i = pl.program_id(1)
    lane = lax.broadcasted_iota(jnp.int32, (TQ, LANES), 1)
    lane_f = lane.astype(F32)
    causal = _causal_bias()
    pair_out = []
    for c in range(WIDTH // LANES):
        cols = slice(c * LANES, (c + 1) * LANES)
        q_pair = q_ref[:, cols]
        km_pair = km_ref[:, cols].astype(BF16)
        head_out = []
        for hh in range(2):
            q_h = jnp.where(_head_lane_mask((TQ, LANES), hh), q_pair, jnp.zeros_like(q_pair))
            gate = jnp.where(lane < i, _dot_t(q_h, km_pair), NEG_INF)
            sel = jnp.zeros((TQ, LANES), F32)
            for _ in range(MOBA_TOPK):
                top = jnp.max(gate, axis=1, keepdims=True)
                first = jnp.min(jnp.where(gate == top, lane_f, float(LANES)), axis=1, keepdims=True)
                hit = (lane_f == first) & (top > NEG_INF)
                sel = jnp.where(hit, 1.0, sel)
                gate = jnp.where(lane_f == first, NEG_INF, gate)

            def kv_step(j, carry, q_h=q_h, sel=sel, cols=cols):
                m, l, acc = carry
                rows = pl.ds(pl.multiple_of(j * TK, TK), TK)
                chosen = jnp.sum(jnp.where(lane == j, sel, 0.0), axis=1, keepdims=True)
                s = _dot_t(q_h, k_ref[rows, cols]) + jnp.where(chosen > 0.0, 0.0, NEG_INF)
                return _softmax_step(s, m, l, acc, v_ref[rows, cols])

            init = (jnp.full((TQ, 1), M_INIT, F32), jnp.zeros((TQ, 1), F32), jnp.zeros((TQ, LANES), F32))
            m, l, acc = lax.fori_loop(0, i, kv_step, init)
            own = pl.ds(pl.multiple_of(i * TK, TK), TK)
            s = _dot_t(q_h, k_ref[own, cols]) + causal
            m, l, acc = _softmax_step(s, m, l, acc, v_ref[own, cols])
            head_out.append(acc / l)
        pair_out.append(jnp.where(_head_lane_mask((TQ, LANES), 0), head_out[0], head_out[1]))
    o = jnp.concatenate(pair_out, axis=1)
    o_ref[...] = _rms(o, g_ref[...]).astype(BF16)


def _moba_call(mq, mk, mv, km, g, batch, seq):
    per_seq = seq // TQ
    kv_spec = pl.BlockSpec((seq, WIDTH), lambda b, i: (b, 0))
    return pl.pallas_call(
        _moba_kernel,
        grid=(batch, per_seq),
        in_specs=[
            pl.BlockSpec((TQ, WIDTH), lambda b, i: (b * per_seq + i, 0)),
            kv_spec, kv_spec,
            pl.BlockSpec((None, LANES, WIDTH), lambda b, i: (b, 0, 0)),
            _resident((1, WIDTH)),
        ],
        out_specs=pl.BlockSpec((TQ, WIDTH), lambda b, i: (b * per_seq + i, 0)),
        out_shape=jax.ShapeDtypeStruct(mq.shape, BF16),
        compiler_params=pltpu.CompilerParams(
            dimension_semantics=("parallel", "parallel"), vmem_limit_bytes=VMEM_LIMIT),
        name="moba",
    )(mq, mk, mv, km, g)


def _dsa_kernel(q_ref, k_ref, v_ref, qi_ref, kib_ref, kw_ref, g_ref, o_ref,
                key_sc, wb_sc, qim_sc, qm_sc, m_sc, l_sc, acc_sc):
    i = pl.program_id(1)
    n_kv = i + 1
    row = lax.broadcasted_iota(jnp.int32, (TQ, TK), 0)
    col = lax.broadcasted_iota(jnp.int32, (TQ, TK), 1)

    kw = kw_ref[...]
    for h in range(N_HEADS):
        c, hh = divmod(h, 2)
        cols = slice(c * LANES, (c + 1) * LANES)
        keep = _head_lane_mask((TQ, LANES), hh)
        qi_pair = qi_ref[:, cols]
        q_pair = q_ref[:, cols]
        qim_sc[h] = jnp.where(keep, qi_pair, jnp.zeros_like(qi_pair))
        qm_sc[h] = jnp.where(keep, q_pair, jnp.zeros_like(q_pair))
        w_col = kw[:, IDX_DIM + h:IDX_DIM + h + 1] * W_SCALE
        wb_sc[h] = jnp.broadcast_to(w_col, (TQ, TK))
        m_sc[h] = jnp.full((TQ, 1), M_INIT, F32)
        l_sc[h] = jnp.zeros((TQ, 1), F32)
        acc_sc[h] = jnp.zeros((TQ, LANES), F32)

    def score_step(j, _):
        rows = pl.ds(pl.multiple_of(j * TK, TK), TK)
        k_idx = kib_ref[rows, :]
        score = jnp.zeros((TQ, TK), F32)
        for h in range(N_IDX_HEADS):
            score = score + wb_sc[h] * jnp.maximum(_dot_t(qim_sc[h], k_idx), 0.0)
        bits = pltpu.bitcast(score + 0.0, jnp.int32)
        key = jnp.where(bits < 0, bits ^ jnp.int32(0x7FFFFFFF), bits)
        admissible = (col <= row) | (j < i)
        key_sc[j] = jnp.where(admissible, key, jnp.int32(INT_MIN))
        return 0

    lax.fori_loop(0, n_kv, score_step, 0)

    def count(pred):
        def step(j, acc):
            return acc + jnp.where(pred(key_sc[j], j), 1.0, 0.0)
        acc = lax.fori_loop(0, n_kv, step, jnp.zeros((TQ, TK), F32))
        return jnp.sum(acc, axis=1, keepdims=True)

    def bit_step(b, thr):
        cand = thr + lax.shift_left(jnp.int32(1), 31 - b)
        cnt = count(lambda k, j: k >= cand)
        return jnp.where(cnt >= float(DSA_TOPK), cand, thr)

    thr = lax.fori_loop(0, 32, bit_step, jnp.full((TQ, 1), INT_MIN, jnp.int32))

    n_ge = count(lambda k, j: k >= thr)
    excess = (n_ge > float(DSA_TOPK)) & (thr > jnp.int32(INT_MIN))

    @pl.when(jnp.max(jnp.where(excess, 1.0, 0.0)) > 0.0)
    def _():
        need = float(DSA_TOPK) - count(lambda k, j: k > thr)

        def idx_step(b, cut):
            cand = cut + lax.shift_left(jnp.int32(1), 12 - b)
            cnt = count(lambda k, j: (k == thr) & (j * TK + col < cand))
            return jnp.where(cnt <= need, cand, cut)

        cut = lax.fori_loop(0, 13, idx_step, jnp.zeros((TQ, 1), jnp.int32))

        def demote(j, _):
            k = key_sc[j]
            drop = excess & (k == thr) & (j * TK + col >= cut)
            key_sc[j] = jnp.where(drop, thr - 1, k)
            return 0

        lax.fori_loop(0, n_kv, demote, 0)

    thr = jnp.maximum(thr, jnp.int32(INT_MIN + 1))

    def attn_step(j, _):
        rows = pl.ds(pl.multiple_of(j * TK, TK), TK)
        bias = jnp.where(key_sc[j] >= thr, 0.0, NEG_INF)
        for h in range(N_HEADS):
            cols = slice((h // 2) * LANES, (h // 2 + 1) * LANES)
            s = _dot_t(qm_sc[h], k_ref[rows, cols]) + bias
            m_sc[h], l_sc[h], acc_sc[h] = _softmax_step(s, m_sc[h], l_sc[h], acc_sc[h], v_ref[rows, cols])
        return 0

    lax.fori_loop(0, n_kv, attn_step, 0)

    pair_out = []
    for c in range(WIDTH // LANES):
        lo = acc_sc[2 * c] / l_sc[2 * c]
        hi = acc_sc[2 * c + 1] / l_sc[2 * c + 1]
        pair_out.append(jnp.where(_head_lane_mask((TQ, LANES), 0), lo, hi))
    o = jnp.concatenate(pair_out, axis=1)
    o_ref[...] = _rms(o, g_ref[...]).astype(BF16)


def _dsa_call(dq, dk, dv, qi, kib, kw, g, batch, seq):
    per_seq = seq // TQ
    q_spec = pl.BlockSpec((TQ, WIDTH), lambda b, i: (b * per_seq + i, 0))
    kv_spec = pl.BlockSpec((seq, WIDTH), lambda b, i: (b, 0))
    return pl.pallas_call(
        _dsa_kernel,
        grid=(batch, per_seq),
        in_specs=[
            q_spec, kv_spec, kv_spec, q_spec,
            pl.BlockSpec((seq, LANES), lambda b, i: (b, 0)),
            pl.BlockSpec((TQ, LANES), lambda b, i: (b * per_seq + i, 0)),
            _resident((1, WIDTH)),
        ],
        out_specs=q_spec,
        out_shape=jax.ShapeDtypeStruct(dq.shape, BF16),
        scratch_shapes=[
            pltpu.VMEM((per_seq, TQ, TK), jnp.int32),
            pltpu.VMEM((N_IDX_HEADS, TQ, TK), F32),
            pltpu.VMEM((N_IDX_HEADS, TQ, LANES), BF16),
            pltpu.VMEM((N_HEADS, TQ, LANES), BF16),
            pltpu.VMEM((N_HEADS, TQ, 1), F32),
            pltpu.VMEM((N_HEADS, TQ, 1), F32),
            pltpu.VMEM((N_HEADS, TQ, LANES), F32),
        ],
        compiler_params=pltpu.CompilerParams(
            dimension_semantics=("parallel", "parallel"), vmem_limit_bytes=VMEM_LIMIT),
        name="dsa",
    )(dq, dk, dv, qi, kib, kw, g)


def _ffn_kernel(x_ref, om_ref, od_ref, mod_ref, g_post_ref, g_pre_ref, g_postf_ref,
                wo_ref, wa_ref, wl_ref, wc_ref, bc_ref, wd_ref, o_ref,
                apad_sc, gated_sc, tail_sc, *, tiles_per_seq):
    tm = x_ref.shape[0]
    first = (pl.program_id(0) % tiles_per_seq) == 0
    mod = mod_ref[...]
    o = jnp.concatenate([om_ref[...], od_ref[...]], axis=1)
    y = jnp.dot(o, wo_ref[...], preferred_element_type=F32)
    x1 = x_ref[...] + mod[2:3] * _rms(y, g_post_ref[...])
    hb = (_rms(x1, g_pre_ref[...]) * (1.0 + mod[4:5]) + mod[3:4]).astype(BF16)
    for c in range(D_FF // FF_CHUNK):
        cols = slice(c * FF_CHUNK, (c + 1) * FF_CHUNK)
        a = jnp.dot(hb, wa_ref[:, cols], preferred_element_type=F32)
        apad_sc[0:SUBLANES, :] = jnp.where(first, 0.0, tail_sc[:, cols])
        apad_sc[SUBLANES:, :] = a
        tail_sc[:, cols] = a[tm - SUBLANES:, :]
        wc = wc_ref[:, cols]
        conv = (wc[0:1] * apad_sc[SUBLANES - 2:SUBLANES - 2 + tm, :]
                + wc[1:2] * apad_sc[SUBLANES - 1:SUBLANES - 1 + tm, :]
                + wc[2:3] * a + bc_ref[:, cols])
        u = jnp.dot(hb, wl_ref[:, cols], preferred_element_type=F32)
        gated_sc[:, cols] = (jax.nn.gelu(conv) * u).astype(BF16)
    y2 = jnp.dot(gated_sc[...], wd_ref[...], preferred_element_type=F32)
    o_ref[...] = x1 + mod[5:6] * _rms(y2, g_postf_ref[...])


def _ffn_call(x2, om, od, mod, g_post, g_pre, g_postf, wo, wa, wl, wc, bc, wd, batch, seq):
    m = x2.shape[0]
    tm = TM_FFN
    per_seq = seq // tm
    row = lambda i: (i, 0)
    return pl.pallas_call(
        functools.partial(_ffn_kernel, tiles_per_seq=per_seq),
        grid=(m // tm,),
        in_specs=[
            pl.BlockSpec((tm, D_MODEL), row),
            pl.BlockSpec((tm, WIDTH), row),
            pl.BlockSpec((tm, WIDTH), row),
            pl.BlockSpec((None, SUBLANES, D_MODEL), lambda i: (i // per_seq, 0, 0)),
            _resident((1, D_MODEL)), _resident((1, D_MODEL)), _resident((1, D_MODEL)),
            _resident((D_MODEL, D_MODEL)),
            _resident((D_MODEL, D_FF)), _resident((D_MODEL, D_FF)),
            _resident((CONV_WIDTH, D_FF)), _resident((1, D_FF)),
            _resident((D_FF, D_MODEL)),
        ],
        out_specs=pl.BlockSpec((tm, D_MODEL), row),
        out_shape=jax.ShapeDtypeStruct(x2.shape, F32),
        scratch_shapes=[
            pltpu.VMEM((tm + SUBLANES, FF_CHUNK), F32),
            pltpu.VMEM((tm, D_FF), BF16),
            pltpu.VMEM((SUBLANES, D_FF), F32),
        ],
        compiler_params=pltpu.CompilerParams(
            dimension_semantics=("arbitrary",), vmem_limit_bytes=VMEM_LIMIT),
        name="ffn",
    )(x2, om, od, mod, g_post, g_pre, g_postf, wo, wa, wl, wc, bc, wd)


def kernel(x, c, w_ada, b_ada, g_pre_mix, w_in, g_moba_out, g_dsa_out, w_out, g_post_mix,
           g_pre_ffn, w_up_act, w_up_lin, w_conv, b_conv, w_down, g_post_ffn):
    batch, seq, d = x.shape
    depth = w_ada.shape[0]
    assert d == D_MODEL and w_in.shape[-1] == PROJ_WIDTH
    assert seq % TM_PROJ == 0 and seq % TM_FFN == 0 and seq % TQ == 0 and seq // MOBA_BLOCK <= LANES
    assert min(DSA_TOPK, seq // 4) == DSA_TOPK

    mod = _ada_call(c, w_ada, b_ada).reshape(depth, batch, N_MOD, D_MODEL)
    mod = jnp.pad(mod, ((0, 0), (0, 0), (0, SUBLANES - N_MOD), (0, 0)))
    tab = _rope_tables(seq)
    vec = lambda g: g.reshape(1, -1)

    x2 = x.reshape(batch * seq, D_MODEL)
    for l in range(depth):
        w_in_b = jnp.pad(w_in[l].astype(BF16), ((0, 0), (0, PROJ_PAD - PROJ_WIDTH)))
        mq, mk, mv, dq, dk, dv, qi, kib, kw, km = _proj_call(
            x2, mod[l], vec(g_pre_mix[l]), w_in_b, tab, batch, seq)
        km = km.reshape(batch, seq // MOBA_BLOCK, WIDTH)
        km = jnp.pad(km, ((0, 0), (0, LANES - seq // MOBA_BLOCK), (0, 0)))
        o_moba = _moba_call(mq, mk, mv, km, vec(g_moba_out[l]), batch, seq)
        o_dsa = _dsa_call(dq, dk, dv, qi, kib, kw, vec(g_dsa_out[l]), batch, seq)
        x2 = _ffn_call(
            x2, o_moba, o_dsa, mod[l], vec(g_post_mix[l]), vec(g_pre_ffn[l]), vec(g_post_ffn[l]),
            w_out[l].astype(BF16), w_up_act[l].astype(BF16), w_up_lin[l].astype(BF16),
            w_conv[l], vec(b_conv[l]), w_down[l].astype(BF16), batch, seq)
    return x2.reshape(batch, seq, D_MODEL)
```

```python
import functools
import math

import jax
import jax.numpy as jnp
from jax import lax
from jax.experimental import pallas as pl
from jax.experimental.pallas import tpu as pltpu

F32 = jnp.float32
BF16 = jnp.bfloat16

D_MODEL = 1024
HEAD_DIM = 64
N_HEADS = 8
WIDTH = N_HEADS * HEAD_DIM
ROPE_THETA = 500000.0
ROPE_DIM = HEAD_DIM // 4
MOBA_BLOCK = 256
MOBA_TOPK = 3
N_IDX_HEADS = 8
IDX_DIM = 64
DSA_TOPK = 256
D_FF = 2816
CONV_WIDTH = 3
NORM_EPS = 1e-6
N_MOD = 6
PROJ_WIDTH = 6 * WIDTH + N_IDX_HEADS * IDX_DIM + IDX_DIM + N_IDX_HEADS

LANES = 128
SUBLANES = 8
BF16_ROWS = 16
PROJ_PAD = 29 * LANES
TAIL_COL = 7 * WIDTH
IDX_SCALE = IDX_DIM ** -0.5
Q_SCALE = HEAD_DIM ** -0.5 * math.log2(math.e)
W_SCALE = N_IDX_HEADS ** -0.5
NEG_INF = float("-inf")
POS_INF = float("inf")
M_INIT = -1e30
INT_MIN = -(2 ** 31)

TM_PROJ = 512
TM_FFN = 512
TQ = MOBA_BLOCK
TK = MOBA_BLOCK
FF_CHUNK = 256
VMEM_LIMIT = 52 * 1024 * 1024

_CONTRACT_LANES = (((1,), (1,)), ((), ()))


def _rms(x, g):
    return x * lax.rsqrt(jnp.mean(x * x, axis=-1, keepdims=True) + NORM_EPS) * g


def _dot_t(a, b):
    return lax.dot_general(a, b, _CONTRACT_LANES, preferred_element_type=F32)


def _resident(shape):
    return pl.BlockSpec(shape, lambda *_: (0,) * len(shape), pipeline_mode=pl.Buffered(1))


def _ada_kernel(c_ref, w_ref, b_ref, o_ref):
    c = c_ref[...]
    c_act = (c * jax.nn.sigmoid(c)).astype(BF16)
    o_ref[...] = jnp.dot(c_act, w_ref[...].astype(BF16), preferred_element_type=F32) + b_ref[...]


def _ada_call(c, w_ada, b_ada):
    depth, d, n = w_ada.shape
    b = c.shape[0]
    tn = n // 4
    return pl.pallas_call(
        _ada_kernel,
        grid=(depth, n // tn),
        in_specs=[
            pl.BlockSpec((b, d), lambda l, j: (0, 0)),
            pl.BlockSpec((None, d, tn), lambda l, j: (l, 0, j)),
            pl.BlockSpec((None, 1, tn), lambda l, j: (l, 0, j)),
        ],
        out_specs=pl.BlockSpec((None, b, tn), lambda l, j: (l, 0, j)),
        out_shape=jax.ShapeDtypeStruct((depth, b, n), F32),
        compiler_params=pltpu.CompilerParams(vmem_limit_bytes=VMEM_LIMIT),
        name="ada",
    )(c, w_ada, b_ada.reshape(depth, 1, n))


def _rope_tables(seq):
    pos = jnp.arange(seq, dtype=F32)
    inv_freq = ROPE_THETA ** (-jnp.arange(0, ROPE_DIM, 2, dtype=F32) / ROPE_DIM)
    ang = pos[:, None] * inv_freq[None, :]
    cos, sin = jnp.cos(ang), jnp.sin(ang)
    half = ROPE_DIM // 2
    d = jnp.arange(LANES) % HEAD_DIM
    f = d % half
    t_self = jnp.where(d < ROPE_DIM, cos[:, f], 1.0)
    t_up = jnp.where(d < half, -sin[:, f], 0.0)
    t_dn = jnp.where((d >= half) & (d < ROPE_DIM), sin[:, f], 0.0)
    return jnp.concatenate([t_self, t_up, t_dn], axis=1).astype(F32)


def _proj_kernel(x_ref, mod_ref, g_ref, w_ref, wvt_ref, tab_ref,
                 mq_ref, mk_ref, mvt_ref, dq_ref, dk_ref, dvt_ref, qi_ref, kib_ref, kw_ref, km_ref):
    mod = mod_ref[...]
    h = _rms(x_ref[...], g_ref[...]) * (1.0 + mod[1:2]) + mod[0:1]
    hb = h.astype(BF16)
    tab = tab_ref[...]
    t_self, t_up, t_dn = tab[:, :LANES], tab[:, LANES:2 * LANES], tab[:, 2 * LANES:]
    half = ROPE_DIM // 2

    def rope(a):
        up = pltpu.roll(a, LANES - half, axis=1)
        dn = pltpu.roll(a, half, axis=1)
        return a * t_self + up * t_up + dn * t_dn

    def roped_group(g_idx, scale):
        a = jnp.dot(hb, w_ref[:, g_idx * WIDTH:(g_idx + 1) * WIDTH], preferred_element_type=F32)
        if scale is not None:
            a = a * scale
        return jnp.concatenate([rope(a[:, c * LANES:(c + 1) * LANES]) for c in range(WIDTH // LANES)], axis=1)

    def values_t(v_idx, out_ref):
        vt = _dot_t(wvt_ref[v_idx], hb).astype(BF16)
        for t in range(vt.shape[1] // TK):
            out_ref[t] = vt[:, t * TK:(t + 1) * TK]

    mq_ref[...] = roped_group(0, Q_SCALE).astype(BF16)
    mk = roped_group(1, None)
    mk_ref[...] = mk.astype(BF16)
    km_ref[...] = jnp.mean(mk.reshape(mk.shape[0] // MOBA_BLOCK, MOBA_BLOCK, WIDTH), axis=1)
    values_t(0, mvt_ref)
    dq_ref[...] = roped_group(3, Q_SCALE).astype(BF16)
    dk_ref[...] = roped_group(4, None).astype(BF16)
    values_t(1, dvt_ref)
    qi_ref[...] = roped_group(6, IDX_SCALE).astype(BF16)

    tail = jnp.dot(hb, w_ref[:, TAIL_COL:TAIL_COL + LANES], preferred_element_type=F32)
    lane = lax.broadcasted_iota(jnp.int32, tail.shape, 1)
    is_key = lane < IDX_DIM
    roped = rope(tail)
    kw_ref[...] = jnp.where(is_key, roped, tail)
    kib_ref[...] = jnp.where(is_key, roped, pltpu.roll(roped, IDX_DIM, axis=1)).astype(BF16)


def _proj_call(x2, mod, g_pre, w_in_b, wvt, tab, batch, seq):
    m = x2.shape[0]
    tm = TM_PROJ
    per_seq = seq // tm
    nblk = tm // MOBA_BLOCK
    row = lambda i: (i, 0)
    lead = lambda i: (i, 0, 0)
    wide = jax.ShapeDtypeStruct((m, WIDTH), BF16)
    wide_t = jax.ShapeDtypeStruct((m // TK, WIDTH, TK), BF16)
    wide_spec = pl.BlockSpec((tm, WIDTH), row)
    wide_t_spec = pl.BlockSpec((tm // TK, WIDTH, TK), lead)
    return pl.pallas_call(
        _proj_kernel,
        grid=(m // tm,),
        in_specs=[
            pl.BlockSpec((tm, D_MODEL), row),
            pl.BlockSpec((None, SUBLANES, D_MODEL), lambda i: (i // per_seq, 0, 0)),
            _resident((1, D_MODEL)),
            _resident((D_MODEL, PROJ_PAD)),
            _resident((2, WIDTH, D_MODEL)),
            pl.BlockSpec((tm, 3 * LANES), lambda i: (i % per_seq, 0)),
        ],
        out_specs=[wide_spec, wide_spec, wide_t_spec, wide_spec, wide_spec, wide_t_spec, wide_spec,
                   pl.BlockSpec((tm, LANES), row),
                   pl.BlockSpec((tm, LANES), row),
                   pl.BlockSpec((None, nblk, WIDTH), lead)],
        out_shape=[wide, wide, wide_t, wide, wide, wide_t, wide,
                   jax.ShapeDtypeStruct((m, LANES), BF16),
                   jax.ShapeDtypeStruct((m, LANES), F32),
                   jax.ShapeDtypeStruct((m // tm, nblk, WIDTH), F32)],
        compiler_params=pltpu.CompilerParams(
            dimension_semantics=("parallel",), vmem_limit_bytes=VMEM_LIMIT),
        name="proj",
    )(x2, mod, g_pre, w_in_b, wvt, tab)


def _mask_heads(q_ref, dst_sc):
    lane = lax.broadcasted_iota(jnp.int32, (TQ, LANES), 1)
    for h in range(N_HEADS):
        c, hh = divmod(h, 2)
        pair = q_ref[:, c * LANES:(c + 1) * LANES]
        keep = (lane < HEAD_DIM) if hh == 0 else (lane >= HEAD_DIM)
        dst_sc[h] = jnp.where(keep, pair, jnp.zeros_like(pair))


def _attend_tile(score_fn, cap_fn, kv_idx, vt_ref, m_sc, l_sc, acc_sc):
    scores = [score_fn(h) for h in range(N_HEADS)]
    weights, decay = [], []
    for h in range(N_HEADS):
        m_cap = cap_fn(h, scores[h])
        m_old = m_sc[h]
        m_new = jnp.maximum(m_old, m_cap)
        p = jnp.exp2(scores[h] - jnp.where(m_cap > NEG_INF, m_new, POS_INF))
        alpha = jnp.exp2(m_old - m_new)
        l_sc[h] = alpha * l_sc[h] + jnp.sum(p, axis=0, keepdims=True)
        m_sc[h] = m_new
        weights.append(p.astype(BF16))
        decay.append(alpha)
    for h in range(N_HEADS):
        vt_tile = vt_ref[kv_idx, h * HEAD_DIM:(h + 1) * HEAD_DIM, :]
        acc_sc[h] = decay[h] * acc_sc[h] + jnp.dot(vt_tile, weights[h], preferred_element_type=F32)


def _init_stats(m_sc, l_sc, acc_sc):
    for h in range(N_HEADS):
        m_sc[h] = jnp.full((1, TQ), M_INIT, F32)
        l_sc[h] = jnp.zeros((1, TQ), F32)
        acc_sc[h] = jnp.zeros((HEAD_DIM, TQ), F32)


def _finish(l_sc, acc_sc, g_ref, o_ref):
    o_t = jnp.concatenate([acc_sc[h] * (1.0 / l_sc[h]) for h in range(N_HEADS)], axis=0)
    o_t = o_t * lax.rsqrt(jnp.mean(o_t * o_t, axis=0, keepdims=True) + NORM_EPS)
    o_ref[...] = (o_t.T * g_ref[...]).astype(BF16)


def _attn_scratch():
    return [
        pltpu.VMEM((N_HEADS, TQ, LANES), BF16),
        pltpu.VMEM((N_HEADS, 1, TQ), F32),
        pltpu.VMEM((N_HEADS, 1, TQ), F32),
        pltpu.VMEM((N_HEADS, HEAD_DIM, TQ), F32),
    ]


def _key_query_iotas():
    return (lax.broadcasted_iota(jnp.int32, (TK, TQ), 0), lax.broadcasted_iota(jnp.int32, (TK, TQ), 1))


def _moba_kernel(q_ref, k_ref, vt_ref, km_ref, g_ref, o_ref, qm_sc, m_sc, l_sc, acc_sc, sel_sc):
    i = pl.program_id(1)
    n_blk = km_ref.shape[0]
    _mask_heads(q_ref, qm_sc)
    _init_stats(m_sc, l_sc, acc_sc)

    blk = lax.broadcasted_iota(jnp.int32, (n_blk, TQ), 0)
    blk_f = blk.astype(F32)
    for h in range(N_HEADS):
        cols = slice((h // 2) * LANES, (h // 2 + 1) * LANES)
        gate = jnp.where(blk < i, _dot_t(km_ref[:, cols].astype(BF16), qm_sc[h]), NEG_INF)
        sel = jnp.zeros((n_blk, TQ), F32)
        for _ in range(MOBA_TOPK):
            top = jnp.max(gate, axis=0, keepdims=True)
            first = jnp.min(jnp.where(gate == top, blk_f, float(n_blk)), axis=0, keepdims=True)
            is_first = blk_f == first
            sel = jnp.where(is_first & (top > NEG_INF), 1.0, sel)
            gate = jnp.where(is_first, NEG_INF, gate)
        sel_sc[h] = sel

    def raw_scores(j):
        rows = pl.ds(pl.multiple_of(j * TK, TK), TK)
        return lambda h: _dot_t(k_ref[rows, (h // 2) * LANES:(h // 2 + 1) * LANES], qm_sc[h])

    def kv_step(j, _):
        def cap(h, s):
            chosen = sel_sc[h, pl.ds(j, 1), :]
            return jnp.where(chosen > 0.0, jnp.max(s, axis=0, keepdims=True), NEG_INF)
        _attend_tile(raw_scores(j), cap, j, vt_ref, m_sc, l_sc, acc_sc)
        return 0

    lax.fori_loop(0, i, kv_step, 0)

    key_pos, query_pos = _key_query_iotas()
    own_scores = raw_scores(i)
    _attend_tile(lambda h: jnp.where(key_pos <= query_pos, own_scores(h), NEG_INF),
                 lambda h, s: jnp.max(s, axis=0, keepdims=True), i, vt_ref, m_sc, l_sc, acc_sc)

    _finish(l_sc, acc_sc, g_ref, o_ref)


def _moba_call(mq, mk, mvt, km, g, batch, seq):
    per_seq = seq // TQ
    n_blk = km.shape[1]
    q_spec = pl.BlockSpec((TQ, WIDTH), lambda b, i: (b * per_seq + i, 0))
    return pl.pallas_call(
        _moba_kernel,
        grid=(batch, per_seq),
        in_specs=[
            q_spec,
            pl.BlockSpec((seq, WIDTH), lambda b, i: (b, 0)),
            pl.BlockSpec((seq // TK, WIDTH, TK), lambda b, i: (b, 0, 0)),
            pl.BlockSpec((None, n_blk, WIDTH), lambda b, i: (b, 0, 0)),
            _resident((1, WIDTH)),
        ],
        out_specs=q_spec,
        out_shape=jax.ShapeDtypeStruct(mq.shape, BF16),
        scratch_shapes=_attn_scratch() + [
            pltpu.VMEM((N_HEADS, n_blk, TQ), F32),
        ],
        compiler_params=pltpu.CompilerParams(
            dimension_semantics=("parallel", "parallel"), vmem_limit_bytes=VMEM_LIMIT),
        name="moba",
    )(mq, mk, mvt, km, g)


def _dsa_kernel(q_ref, k_ref, vt_ref, qi_ref, kib_ref, kw_ref, g_ref, o_ref,
                qm_sc, m_sc, l_sc, acc_sc, qim_sc, w_sc, key_sc):
    i = pl.program_id(1)
    n_kv = i + 1
    key_pos, query_pos = _key_query_iotas()
    _mask_heads(q_ref, qm_sc)
    _mask_heads(qi_ref, qim_sc)
    _init_stats(m_sc, l_sc, acc_sc)
    w_sc[...] = kw_ref[...].T[IDX_DIM:IDX_DIM + N_IDX_HEADS, :] * W_SCALE

    def score_step(j, _):
        rows = pl.ds(pl.multiple_of(j * TK, TK), TK)
        k_idx = kib_ref[rows, :]
        score = jnp.zeros((TK, TQ), F32)
        for h in range(N_IDX_HEADS):
            score = score + w_sc[h:h + 1, :] * jnp.maximum(_dot_t(k_idx, qim_sc[h]), 0.0)
        bits = pltpu.bitcast(score + 0.0, jnp.int32)
        key = jnp.where(bits < 0, bits ^ jnp.int32(0x7FFFFFFF), bits)
        admissible = (key_pos <= query_pos) | (j < i)
        key_sc[j] = jnp.where(admissible, key, jnp.int32(INT_MIN))
        return 0

    lax.fori_loop(0, n_kv, score_step, 0)

    def count(pred):
        def step(j, acc):
            return acc + jnp.sum(jnp.where(pred(key_sc[j], j), 1.0, 0.0), axis=0, keepdims=True)
        return lax.fori_loop(0, n_kv, step, jnp.zeros((1, TQ), F32))

    def bit_step(b, thr):
        cand = thr + lax.shift_left(jnp.int32(1), 31 - b)
        cnt = count(lambda k, j: k >= cand)
        return jnp.where(cnt >= float(DSA_TOPK), cand, thr)

    thr = lax.fori_loop(0, 32, bit_step, jnp.full((1, TQ), INT_MIN, jnp.int32))

    n_ge = count(lambda k, j: k >= thr)
    excess = (n_ge > float(DSA_TOPK)) & (thr > jnp.int32(INT_MIN))

    @pl.when(jnp.max(jnp.where(excess, 1.0, 0.0)) > 0.0)
    def _():
        need = float(DSA_TOPK) - count(lambda k, j: k > thr)

        def idx_step(b, cut):
            cand = cut + lax.shift_left(jnp.int32(1), 12 - b)
            cnt = count(lambda k, j: (k == thr) & (j * TK + key_pos < cand))
            return jnp.where(cnt <= need, cand, cut)

        cut = lax.fori_loop(0, 13, idx_step, jnp.zeros((1, TQ), jnp.int32))

        def demote(j, _):
            k = key_sc[j]
            drop = excess & (k == thr) & (j * TK + key_pos >= cut)
            key_sc[j] = jnp.where(drop, thr - 1, k)
            return 0

        lax.fori_loop(0, n_kv, demote, 0)

    thr = jnp.maximum(thr, jnp.int32(INT_MIN + 1))

    def attn_step(j, _):
        rows = pl.ds(pl.multiple_of(j * TK, TK), TK)
        bias = jnp.where(key_sc[j] >= thr, 0.0, NEG_INF)
        _attend_tile(lambda h: _dot_t(k_ref[rows, (h // 2) * LANES:(h // 2 + 1) * LANES], qm_sc[h]) + bias,
                     lambda h, s: jnp.max(s, axis=0, keepdims=True), j, vt_ref, m_sc, l_sc, acc_sc)
        return 0

    lax.fori_loop(0, n_kv, attn_step, 0)
    _finish(l_sc, acc_sc, g_ref, o_ref)


def _dsa_call(dq, dk, dvt, qi, kib, kw, g, batch, seq):
    per_seq = seq // TQ
    q_spec = pl.BlockSpec((TQ, WIDTH), lambda b, i: (b * per_seq + i, 0))
    return pl.pallas_call(
        _dsa_kernel,
        grid=(batch, per_seq),
        in_specs=[
            q_spec,
            pl.BlockSpec((seq, WIDTH), lambda b, i: (b, 0)),
            pl.BlockSpec((seq // TK, WIDTH, TK), lambda b, i: (b, 0, 0)),
            q_spec,
            pl.BlockSpec((seq, LANES), lambda b, i: (b, 0)),
            pl.BlockSpec((TQ, LANES), lambda b, i: (b * per_seq + i, 0)),
            _resident((1, WIDTH)),
        ],
        out_specs=q_spec,
        out_shape=jax.ShapeDtypeStruct(dq.shape, BF16),
        scratch_shapes=_attn_scratch() + [
            pltpu.VMEM((N_IDX_HEADS, TQ, LANES), BF16),
            pltpu.VMEM((N_IDX_HEADS, TQ), F32),
            pltpu.VMEM((per_seq, TK, TQ), jnp.int32),
        ],
        compiler_params=pltpu.CompilerParams(
            dimension_semantics=("parallel", "parallel"), vmem_limit_bytes=VMEM_LIMIT),
        name="dsa",
    )(dq, dk, dvt, qi, kib, kw, g)


def _ffn_kernel(x_ref, om_ref, od_ref, mod_ref, g_post_ref, g_pre_ref, g_postf_ref,
                wo_ref, wa_ref, wl_ref, wc_ref, bc_ref, wd_ref, o_ref,
                apad_sc, gated_sc, tail_sc, *, tiles_per_seq):
    tm = x_ref.shape[0]
    first = (pl.program_id(0) % tiles_per_seq) == 0
    mod = mod_ref[...]
    o = jnp.concatenate([om_ref[...], od_ref[...]], axis=1)
    y = jnp.dot(o, wo_ref[...], preferred_element_type=F32)
    x1 = x_ref[...] + mod[2:3] * _rms(y, g_post_ref[...])
    hb = (_rms(x1, g_pre_ref[...]) * (1.0 + mod[4:5]) + mod[3:4]).astype(BF16)
    for c in range(D_FF // FF_CHUNK):
        cols = slice(c * FF_CHUNK, (c + 1) * FF_CHUNK)
        a = jnp.dot(hb, wa_ref[:, cols], preferred_element_type=F32)
        apad_sc[0:SUBLANES, :] = jnp.where(first, 0.0, tail_sc[:, cols])
        apad_sc[SUBLANES:, :] = a
        tail_sc[:, cols] = a[tm - SUBLANES:, :]
        wc = wc_ref[:, cols]
        conv = (wc[0:1] * apad_sc[SUBLANES - 2:SUBLANES - 2 + tm, :]
                + wc[1:2] * apad_sc[SUBLANES - 1:SUBLANES - 1 + tm, :]
                + wc[2:3] * a + bc_ref[:, cols])
        u = jnp.dot(hb, wl_ref[:, cols], preferred_element_type=F32)
        gated_sc[:, cols] = (jax.nn.gelu(conv) * u).astype(BF16)
    y2 = jnp.dot(gated_sc[...], wd_ref[...], preferred_element_type=F32)
    o_ref[...] = x1 + mod[5:6] * _rms(y2, g_postf_ref[...])


def _ffn_call(x2, om, od, mod, g_post, g_pre, g_postf, wo, wa, wl, wc, bc, wd, batch, seq):
    m = x2.shape[0]
    tm = TM_FFN
    per_seq = seq // tm
    row = lambda i: (i, 0)
    return pl.pallas_call(
        functools.partial(_ffn_kernel, tiles_per_seq=per_seq),
        grid=(m // tm,),
        in_specs=[
            pl.BlockSpec((tm, D_MODEL), row),
            pl.BlockSpec((tm, WIDTH), row),
            pl.BlockSpec((tm, WIDTH), row),
            pl.BlockSpec((None, SUBLANES, D_MODEL), lambda i: (i // per_seq, 0, 0)),
            _resident((1, D_MODEL)), _resident((1, D_MODEL)), _resident((1, D_MODEL)),
            _resident((D_MODEL, D_MODEL)),
            _resident((D_MODEL, D_FF)), _resident((D_MODEL, D_FF)),
            _resident((CONV_WIDTH, D_FF)), _resident((1, D_FF)),
            _resident((D_FF, D_MODEL)),
        ],
        out_specs=pl.BlockSpec((tm, D_MODEL), row),
        out_shape=jax.ShapeDtypeStruct(x2.shape, F32),
        scratch_shapes=[
            pltpu.VMEM((tm + SUBLANES, FF_CHUNK), F32),
            pltpu.VMEM((tm, D_FF), BF16),
            pltpu.VMEM((SUBLANES, D_FF), F32),
        ],
        compiler_params=pltpu.CompilerParams(
            dimension_semantics=("arbitrary",), vmem_limit_bytes=VMEM_LIMIT),
        name="ffn",
    )(x2, om, od, mod, g_post, g_pre, g_postf, wo, wa, wl, wc, bc, wd)


def kernel(x, c, w_ada, b_ada, g_pre_mix, w_in, g_moba_out, g_dsa_out, w_out, g_post_mix,
           g_pre_ffn, w_up_act, w_up_lin, w_conv, b_conv, w_down, g_post_ffn):
    batch, seq, d = x.shape
    depth = w_ada.shape[0]
    n_blk = seq // MOBA_BLOCK
    assert d == D_MODEL and w_in.shape[-1] == PROJ_WIDTH
    assert seq % TM_PROJ == 0 and seq % TM_FFN == 0 and seq % TQ == 0
    assert min(DSA_TOPK, seq // 4) == DSA_TOPK and seq < 2 ** 13

    mod = _ada_call(c, w_ada, b_ada).reshape(depth, batch, N_MOD, D_MODEL)
    mod = jnp.pad(mod, ((0, 0), (0, 0), (0, SUBLANES - N_MOD), (0, 0)))
    tab = _rope_tables(seq)
    vec = lambda g: g.reshape(1, -1)

    x2 = x.reshape(batch * seq, D_MODEL)
    for l in range(depth):
        w_in_b = jnp.pad(w_in[l].astype(BF16), ((0, 0), (0, PROJ_PAD - PROJ_WIDTH)))
        wvt = jnp.stack([w_in[l][:, 2 * WIDTH:3 * WIDTH].T, w_in[l][:, 5 * WIDTH:6 * WIDTH].T]).astype(BF16)
        mq, mk, mvt, dq, dk, dvt, qi, kib, kw, km = _proj_call(
            x2, mod[l], vec(g_pre_mix[l]), w_in_b, wvt, tab, batch, seq)
        km = km.reshape(batch, n_blk, WIDTH)
        km = jnp.pad(km, ((0, 0), (0, -n_blk % BF16_ROWS), (0, 0)))
        o_moba = _moba_call(mq, mk, mvt, km, vec(g_moba_out[l]), batch, seq)
        o_dsa = _dsa_call(dq, dk, dvt, qi, kib, kw, vec(g_dsa_out[l]), batch, seq)
        x2 = _ffn_call(
            x2, o_moba, o_dsa, mod[l], vec(g_post_mix[l]), vec(g_pre_ffn[l]), vec(g_post_ffn[l]),
            w_out[l].astype(BF16), w_up_act[l].astype(BF16), w_up_lin[l].astype(BF16),
            w_conv[l], vec(b_conv[l]), w_down[l].astype(BF16), batch, seq)
    return x2.reshape(batch, seq, D_MODEL)
```

```python
import functools
import math

import jax
import jax.numpy as jnp
from jax import lax
from jax.experimental import pallas as pl
from jax.experimental.pallas import tpu as pltpu

F32 = jnp.float32
BF16 = jnp.bfloat16

D_MODEL = 1024
HEAD_DIM = 64
N_HEADS = 8
WIDTH = N_HEADS * HEAD_DIM
ROPE_THETA = 500000.0
ROPE_DIM = HEAD_DIM // 4
MOBA_BLOCK = 256
MOBA_TOPK = 3
N_IDX_HEADS = 8
IDX_DIM = 64
DSA_TOPK = 256
D_FF = 2816
CONV_WIDTH = 3
NORM_EPS = 1e-6
N_MOD = 6
PROJ_WIDTH = 6 * WIDTH + N_IDX_HEADS * IDX_DIM + IDX_DIM + N_IDX_HEADS

LANES = 128
SUBLANES = 8
BF16_ROWS = 16
PROJ_PAD = 29 * LANES
TAIL_COL = 7 * WIDTH
IDX_SCALE = IDX_DIM ** -0.5
Q_SCALE = HEAD_DIM ** -0.5 * math.log2(math.e)
W_SCALE = N_IDX_HEADS ** -0.5
NEG_INF = float("-inf")
POS_INF = float("inf")
M_INIT = -1e30
INT_MIN = -(2 ** 31)
HALF_MIN = -(2 ** 15)

TM_PROJ = 512
TM_FFN = 512
TQ = MOBA_BLOCK
TK = MOBA_BLOCK
FF_CHUNK = 256
HEAD_LOOKAHEAD = 4
VMEM_LIMIT = 52 * 1024 * 1024

_CONTRACT_LANES = (((1,), (1,)), ((), ()))


def _rms(x, g):
    return x * lax.rsqrt(jnp.mean(x * x, axis=-1, keepdims=True) + NORM_EPS) * g


def _dot_t(a, b):
    return lax.dot_general(a, b, _CONTRACT_LANES, preferred_element_type=F32)


def _resident(shape):
    return pl.BlockSpec(shape, lambda *_: (0,) * len(shape), pipeline_mode=pl.Buffered(1))


def _ada_kernel(c_ref, w_ref, b_ref, o_ref):
    c = c_ref[...]
    c_act = (c * jax.nn.sigmoid(c)).astype(BF16)
    o_ref[...] = jnp.dot(c_act, w_ref[...].astype(BF16), preferred_element_type=F32) + b_ref[...]


def _ada_call(c, w_ada, b_ada):
    depth, d, n = w_ada.shape
    b = c.shape[0]
    tn = n // 4
    return pl.pallas_call(
        _ada_kernel,
        grid=(depth, n // tn),
        in_specs=[
            pl.BlockSpec((b, d), lambda l, j: (0, 0)),
            pl.BlockSpec((None, d, tn), lambda l, j: (l, 0, j)),
            pl.BlockSpec((None, 1, tn), lambda l, j: (l, 0, j)),
        ],
        out_specs=pl.BlockSpec((None, b, tn), lambda l, j: (l, 0, j)),
        out_shape=jax.ShapeDtypeStruct((depth, b, n), F32),
        compiler_params=pltpu.CompilerParams(vmem_limit_bytes=VMEM_LIMIT),
        name="ada",
    )(c, w_ada, b_ada.reshape(depth, 1, n))


def _rope_tables(seq):
    pos = jnp.arange(seq, dtype=F32)
    inv_freq = ROPE_THETA ** (-jnp.arange(0, ROPE_DIM, 2, dtype=F32) / ROPE_DIM)
    ang = pos[:, None] * inv_freq[None, :]
    cos, sin = jnp.cos(ang), jnp.sin(ang)
    half = ROPE_DIM // 2
    d = jnp.arange(LANES) % HEAD_DIM
    f = d % half
    t_self = jnp.where(d < ROPE_DIM, cos[:, f], 1.0)
    t_up = jnp.where(d < half, -sin[:, f], 0.0)
    t_dn = jnp.where((d >= half) & (d < ROPE_DIM), sin[:, f], 0.0)
    return jnp.concatenate([t_self, t_up, t_dn], axis=1).astype(F32)


def _proj_kernel(x_ref, mod_ref, g_ref, w_ref, wvt_ref, tab_ref,
                 mq_ref, mk_ref, mvt_ref, dq_ref, dk_ref, dvt_ref, qi_ref, kib_ref, kw_ref, km_ref):
    mod = mod_ref[...]
    h = _rms(x_ref[...], g_ref[...]) * (1.0 + mod[1:2]) + mod[0:1]
    hb = h.astype(BF16)
    tab = tab_ref[...]
    t_self, t_up, t_dn = tab[:, :LANES], tab[:, LANES:2 * LANES], tab[:, 2 * LANES:]
    half = ROPE_DIM // 2

    def rope(a):
        up = pltpu.roll(a, LANES - half, axis=1)
        dn = pltpu.roll(a, half, axis=1)
        return a * t_self + up * t_up + dn * t_dn

    def roped_group(g_idx, scale):
        a = jnp.dot(hb, w_ref[:, g_idx * WIDTH:(g_idx + 1) * WIDTH], preferred_element_type=F32)
        if scale is not None:
            a = a * scale
        return jnp.concatenate([rope(a[:, c * LANES:(c + 1) * LANES]) for c in range(WIDTH // LANES)], axis=1)

    def values_t(v_idx, out_ref):
        vt = _dot_t(wvt_ref[v_idx], hb).astype(BF16)
        for t in range(vt.shape[1] // TK):
            out_ref[t] = vt[:, t * TK:(t + 1) * TK]

    mq_ref[...] = roped_group(0, Q_SCALE).astype(BF16)
    mk = roped_group(1, None)
    mk_ref[...] = mk.astype(BF16)
    km_ref[...] = jnp.mean(mk.reshape(mk.shape[0] // MOBA_BLOCK, MOBA_BLOCK, WIDTH), axis=1)
    values_t(0, mvt_ref)
    dq_ref[...] = roped_group(3, Q_SCALE).astype(BF16)
    dk_ref[...] = roped_group(4, None).astype(BF16)
    values_t(1, dvt_ref)
    qi_ref[...] = roped_group(6, IDX_SCALE).astype(BF16)

    tail = jnp.dot(hb, w_ref[:, TAIL_COL:TAIL_COL + LANES], preferred_element_type=F32)
    lane = lax.broadcasted_iota(jnp.int32, tail.shape, 1)
    is_key = lane < IDX_DIM
    roped = rope(tail)
    kw_ref[...] = jnp.where(is_key, roped, tail)
    kib_ref[...] = jnp.where(is_key, roped, pltpu.roll(roped, IDX_DIM, axis=1)).astype(BF16)


def _proj_call(x2, mod, g_pre, w_in_b, wvt, tab, batch, seq):
    m = x2.shape[0]
    tm = TM_PROJ
    per_seq = seq // tm
    nblk = tm // MOBA_BLOCK
    row = lambda i: (i, 0)
    lead = lambda i: (i, 0, 0)
    wide = jax.ShapeDtypeStruct((m, WIDTH), BF16)
    wide_t = jax.ShapeDtypeStruct((m // TK, WIDTH, TK), BF16)
    wide_spec = pl.BlockSpec((tm, WIDTH), row)
    wide_t_spec = pl.BlockSpec((tm // TK, WIDTH, TK), lead)
    return pl.pallas_call(
        _proj_kernel,
        grid=(m // tm,),
        in_specs=[
            pl.BlockSpec((tm, D_MODEL), row),
            pl.BlockSpec((None, SUBLANES, D_MODEL), lambda i: (i // per_seq, 0, 0)),
            _resident((1, D_MODEL)),
            _resident((D_MODEL, PROJ_PAD)),
            _resident((2, WIDTH, D_MODEL)),
            pl.BlockSpec((tm, 3 * LANES), lambda i: (i % per_seq, 0)),
        ],
        out_specs=[wide_spec, wide_spec, wide_t_spec, wide_spec, wide_spec, wide_t_spec, wide_spec,
                   pl.BlockSpec((tm, LANES), row),
                   pl.BlockSpec((tm, LANES), row),
                   pl.BlockSpec((None, nblk, WIDTH), lead)],
        out_shape=[wide, wide, wide_t, wide, wide, wide_t, wide,
                   jax.ShapeDtypeStruct((m, LANES), BF16),
                   jax.ShapeDtypeStruct((m, LANES), F32),
                   jax.ShapeDtypeStruct((m // tm, nblk, WIDTH), F32)],
        compiler_params=pltpu.CompilerParams(
            dimension_semantics=("parallel",), vmem_limit_bytes=VMEM_LIMIT),
        name="proj",
    )(x2, mod, g_pre, w_in_b, wvt, tab)


def _mask_heads(q_ref, dst_sc):
    lane = lax.broadcasted_iota(jnp.int32, (TQ, LANES), 1)
    for h in range(N_HEADS):
        c, hh = divmod(h, 2)
        pair = q_ref[:, c * LANES:(c + 1) * LANES]
        keep = (lane < HEAD_DIM) if hh == 0 else (lane >= HEAD_DIM)
        dst_sc[h] = jnp.where(keep, pair, jnp.zeros_like(pair))


def _attend_tile(score_fn, cap_fn, kv_idx, n_sub, vt_ref, m_sc, acc_sc):
    ones_rows = jnp.ones((BF16_ROWS, n_sub * TK), BF16)
    scores = {h: score_fn(h) for h in range(min(HEAD_LOOKAHEAD, N_HEADS))}
    for h in range(N_HEADS):
        s = scores.pop(h)
        subs = [s[t * TK:(t + 1) * TK] for t in range(n_sub)]
        caps = [cap_fn(h, t, subs[t]) for t in range(n_sub)]
        m_old = m_sc[h]
        m_new = functools.reduce(jnp.maximum, caps, m_old)
        p = jnp.concatenate(
            [jnp.exp2((subs[t] - jnp.where(caps[t] > NEG_INF, m_new, POS_INF)).astype(BF16)) for t in range(n_sub)],
            axis=0)
        alpha = jnp.exp2(m_old - m_new)
        m_sc[h] = m_new
        if h + HEAD_LOOKAHEAD < N_HEADS:
            scores[h + HEAD_LOOKAHEAD] = score_fn(h + HEAD_LOOKAHEAD)
        values_t = jnp.concatenate(
            [vt_ref[kv_idx + t, h * HEAD_DIM:(h + 1) * HEAD_DIM, :] for t in range(n_sub)], axis=1)
        vt_tile = jnp.concatenate([values_t, ones_rows], axis=0)
        acc_sc[h] = alpha * acc_sc[h] + jnp.dot(vt_tile, p, preferred_element_type=F32)


def _over_key_tiles(n_tiles, step):
    def pair(jj, _):
        step(2 * jj, 2)
        return 0

    lax.fori_loop(0, n_tiles // 2, pair, 0)

    @pl.when(n_tiles % 2 == 1)
    def _():
        step(n_tiles - 1, 1)


def _init_stats(m_sc, acc_sc):
    for h in range(N_HEADS):
        m_sc[h] = jnp.full((1, TQ), M_INIT, F32)
        acc_sc[h] = jnp.zeros((HEAD_DIM + BF16_ROWS, TQ), F32)


def _finish(acc_sc, g_ref, o_ref):
    o_t = jnp.concatenate(
        [acc_sc[h, :HEAD_DIM, :] * (1.0 / acc_sc[h, HEAD_DIM:HEAD_DIM + 1, :]) for h in range(N_HEADS)], axis=0)
    o_t = o_t * lax.rsqrt(jnp.mean(o_t * o_t, axis=0, keepdims=True) + NORM_EPS)
    o_ref[...] = (o_t.T * g_ref[...]).astype(BF16)


def _attn_scratch():
    return [
        pltpu.VMEM((N_HEADS, TQ, LANES), BF16),
        pltpu.VMEM((N_HEADS, 1, TQ), F32),
        pltpu.VMEM((N_HEADS, HEAD_DIM + BF16_ROWS, TQ), F32),
    ]


def _key_query_iotas():
    return (lax.broadcasted_iota(jnp.int32, (TK, TQ), 0), lax.broadcasted_iota(jnp.int32, (TK, TQ), 1))


def _moba_kernel(q_ref, k_ref, vt_ref, km_ref, g_ref, o_ref, qm_sc, m_sc, acc_sc, sel_sc):
    i = pl.program_id(1)
    n_blk = km_ref.shape[0]
    _mask_heads(q_ref, qm_sc)
    _init_stats(m_sc, acc_sc)

    blk = lax.broadcasted_iota(jnp.int32, (n_blk, TQ), 0)
    blk_f = blk.astype(F32)
    for h in range(N_HEADS):
        cols = slice((h // 2) * LANES, (h // 2 + 1) * LANES)
        gate = jnp.where(blk < i, _dot_t(km_ref[:, cols].astype(BF16), qm_sc[h]), NEG_INF)
        sel = jnp.zeros((n_blk, TQ), F32)
        for _ in range(MOBA_TOPK):
            top = jnp.max(gate, axis=0, keepdims=True)
            first = jnp.min(jnp.where(gate == top, blk_f, float(n_blk)), axis=0, keepdims=True)
            is_first = blk_f == first
            sel = jnp.where(is_first & (top > NEG_INF), 1.0, sel)
            gate = jnp.where(is_first, NEG_INF, gate)
        sel_sc[h] = sel

    def raw_scores(j, n_sub):
        rows = pl.ds(pl.multiple_of(j * TK, TK), n_sub * TK)
        return lambda h: _dot_t(k_ref[rows, (h // 2) * LANES:(h // 2 + 1) * LANES], qm_sc[h])

    def past_step(j, n_sub):
        def cap(h, t, s):
            chosen = sel_sc[h, pl.ds(j + t, 1), :]
            return jnp.where(chosen > 0.0, jnp.max(s, axis=0, keepdims=True), NEG_INF)
        _attend_tile(raw_scores(j, n_sub), cap, j, n_sub, vt_ref, m_sc, acc_sc)

    _over_key_tiles(i, past_step)

    key_pos, query_pos = _key_query_iotas()
    own_scores = raw_scores(i, 1)
    _attend_tile(lambda h: jnp.where(key_pos <= query_pos, own_scores(h), NEG_INF),
                 lambda h, t, s: jnp.max(s, axis=0, keepdims=True), i, 1, vt_ref, m_sc, acc_sc)

    _finish(acc_sc, g_ref, o_ref)


def _moba_call(mq, mk, mvt, km, g, batch, seq):
    per_seq = seq // TQ
    n_blk = km.shape[1]
    q_spec = pl.BlockSpec((TQ, WIDTH), lambda b, i: (b * per_seq + i, 0))
    return pl.pallas_call(
        _moba_kernel,
        grid=(batch, per_seq),
        in_specs=[
            q_spec,
            pl.BlockSpec((seq, WIDTH), lambda b, i: (b, 0)),
            pl.BlockSpec((seq // TK, WIDTH, TK), lambda b, i: (b, 0, 0)),
            pl.BlockSpec((None, n_blk, WIDTH), lambda b, i: (b, 0, 0)),
            _resident((1, WIDTH)),
        ],
        out_specs=q_spec,
        out_shape=jax.ShapeDtypeStruct(mq.shape, BF16),
        scratch_shapes=_attn_scratch() + [
            pltpu.VMEM((N_HEADS, n_blk, TQ), F32),
        ],
        compiler_params=pltpu.CompilerParams(
            dimension_semantics=("parallel", "parallel"), vmem_limit_bytes=VMEM_LIMIT),
        name="moba",
    )(mq, mk, mvt, km, g)


def _dsa_kernel(q_ref, k_ref, vt_ref, qi_ref, kib_ref, kw_ref, g_ref, o_ref,
                qm_sc, m_sc, acc_sc, qim_sc, w_sc, key_sc, half_sc):
    i = pl.program_id(1)
    n_kv = i + 1
    key_pos, query_pos = _key_query_iotas()
    _mask_heads(q_ref, qm_sc)
    _mask_heads(qi_ref, qim_sc)
    _init_stats(m_sc, acc_sc)
    w_sc[...] = kw_ref[...].T[IDX_DIM:IDX_DIM + N_IDX_HEADS, :] * W_SCALE

    def score_step(j, _):
        rows = pl.ds(pl.multiple_of(j * TK, TK), TK)
        k_idx = kib_ref[rows, :]
        score = jnp.zeros((TK, TQ), F32)
        for h in range(N_IDX_HEADS):
            score = score + w_sc[h:h + 1, :] * jnp.maximum(_dot_t(k_idx, qim_sc[h]), 0.0)
        bits = pltpu.bitcast(score + 0.0, jnp.int32)
        key = jnp.where(bits < 0, bits ^ jnp.int32(0x7FFFFFFF), bits)
        admissible = (key_pos <= query_pos) | (j < i)
        key = jnp.where(admissible, key, jnp.int32(INT_MIN))
        key_sc[j] = key
        half_sc[j] = lax.shift_right_arithmetic(key, 16).astype(jnp.int16)
        return 0

    lax.fori_loop(0, n_kv, score_step, 0)

    def count(pred):
        def step(j, acc):
            return acc + jnp.sum(jnp.where(pred(key_sc[j], j), 1.0, 0.0), axis=0, keepdims=True)
        return lax.fori_loop(0, n_kv, step, jnp.zeros((1, TQ), F32))

    def count_half(pred):
        def step(j, acc):
            ones = jnp.where(pred(half_sc[j]), jnp.int16(1), jnp.int16(0))
            for r in range(TK // BF16_ROWS):
                acc = acc + ones[r * BF16_ROWS:(r + 1) * BF16_ROWS]
            return acc
        acc = lax.fori_loop(0, n_kv, step, jnp.zeros((BF16_ROWS, TQ), jnp.int16))
        return jnp.sum(acc.astype(jnp.int32), axis=0, keepdims=True)

    def bisect_half(need):
        def bit_step(b, thr):
            cand = thr + lax.shift_left(jnp.int32(1), 15 - b)
            cand16 = cand.astype(jnp.int16)
            cnt = count_half(lambda half: half >= cand16)
            return jnp.where(cnt >= need, cand, thr)
        return lax.fori_loop(0, 16, bit_step, jnp.full((1, TQ), HALF_MIN, jnp.int32))

    hi = bisect_half(jnp.full((1, TQ), DSA_TOPK, jnp.int32))
    hi16 = hi.astype(jnp.int16)
    need_lo = DSA_TOPK - count_half(lambda half: half > hi16)

    def to_lower_half(j, _):
        lower = ((key_sc[j] & jnp.int32(0xFFFF)) + jnp.int32(HALF_MIN)).astype(jnp.int16)
        half_sc[j] = jnp.where(half_sc[j] == hi16, lower, jnp.int16(HALF_MIN))
        return 0

    lax.fori_loop(0, n_kv, to_lower_half, 0)
    lo = bisect_half(need_lo)
    thr = hi * jnp.int32(2 ** 16) + (lo - jnp.int32(HALF_MIN))

    n_ge = count(lambda k, j: k >= thr)
    excess = (n_ge > float(DSA_TOPK)) & (thr > jnp.int32(INT_MIN))

    @pl.when(jnp.max(jnp.where(excess, 1.0, 0.0)) > 0.0)
    def _():
        need = float(DSA_TOPK) - count(lambda k, j: k > thr)

        def idx_step(b, cut):
            cand = cut + lax.shift_left(jnp.int32(1), 12 - b)
            cnt = count(lambda k, j: (k == thr) & (j * TK + key_pos < cand))
            return jnp.where(cnt <= need, cand, cut)

        cut = lax.fori_loop(0, 13, idx_step, jnp.zeros((1, TQ), jnp.int32))

        def demote(j, _):
            k = key_sc[j]
            drop = excess & (k == thr) & (j * TK + key_pos >= cut)
            key_sc[j] = jnp.where(drop, thr - 1, k)
            return 0

        lax.fori_loop(0, n_kv, demote, 0)

    thr = jnp.maximum(thr, jnp.int32(INT_MIN + 1))

    def attn_step(j, n_sub):
        rows = pl.ds(pl.multiple_of(j * TK, TK), n_sub * TK)
        bias = jnp.concatenate(
            [jnp.where(key_sc[j + t] >= thr, 0.0, NEG_INF) for t in range(n_sub)], axis=0)
        _attend_tile(lambda h: _dot_t(k_ref[rows, (h // 2) * LANES:(h // 2 + 1) * LANES], qm_sc[h]) + bias,
                     lambda h, t, s: jnp.max(s, axis=0, keepdims=True), j, n_sub, vt_ref, m_sc, acc_sc)

    _over_key_tiles(n_kv, attn_step)
    _finish(acc_sc, g_ref, o_ref)


def _dsa_call(dq, dk, dvt, qi, kib, kw, g, batch, seq):
    per_seq = seq // TQ
    q_spec = pl.BlockSpec((TQ, WIDTH), lambda b, i: (b * per_seq + i, 0))
    return pl.pallas_call(
        _dsa_kernel,
        grid=(batch, per_seq),
        in_specs=[
            q_spec,
            pl.BlockSpec((seq, WIDTH), lambda b, i: (b, 0)),
            pl.BlockSpec((seq // TK, WIDTH, TK), lambda b, i: (b, 0, 0)),
            q_spec,
            pl.BlockSpec((seq, LANES), lambda b, i: (b, 0)),
            pl.BlockSpec((TQ, LANES), lambda b, i: (b * per_seq + i, 0)),
            _resident((1, WIDTH)),
        ],
        out_specs=q_spec,
        out_shape=jax.ShapeDtypeStruct(dq.shape, BF16),
        scratch_shapes=_attn_scratch() + [
            pltpu.VMEM((N_IDX_HEADS, TQ, LANES), BF16),
            pltpu.VMEM((N_IDX_HEADS, TQ), F32),
            pltpu.VMEM((per_seq, TK, TQ), jnp.int32),
            pltpu.VMEM((per_seq, TK, TQ), jnp.int16),
        ],
        compiler_params=pltpu.CompilerParams(
            dimension_semantics=("parallel", "parallel"), vmem_limit_bytes=VMEM_LIMIT),
        name="dsa",
    )(dq, dk, dvt, qi, kib, kw, g)


def _ffn_kernel(x_ref, om_ref, od_ref, mod_ref, g_post_ref, g_pre_ref, g_postf_ref,
                wo_ref, wa_ref, wl_ref, wc_ref, bc_ref, wd_ref, o_ref,
                apad_sc, gated_sc, tail_sc, *, tiles_per_seq):
    tm = x_ref.shape[0]
    first = (pl.program_id(0) % tiles_per_seq) == 0
    mod = mod_ref[...]
    o = jnp.concatenate([om_ref[...], od_ref[...]], axis=1)
    y = jnp.dot(o, wo_ref[...], preferred_element_type=F32)
    x1 = x_ref[...] + mod[2:3] * _rms(y, g_post_ref[...])
    hb = (_rms(x1, g_pre_ref[...]) * (1.0 + mod[4:5]) + mod[3:4]).astype(BF16)
    for c in range(D_FF // FF_CHUNK):
        cols = slice(c * FF_CHUNK, (c + 1) * FF_CHUNK)
        a = jnp.dot(hb, wa_ref[:, cols], preferred_element_type=F32)
        apad_sc[0:SUBLANES, :] = jnp.where(first, 0.0, tail_sc[:, cols])
        apad_sc[SUBLANES:, :] = a
        tail_sc[:, cols] = a[tm - SUBLANES:, :]
        wc = wc_ref[:, cols]
        conv = (wc[0:1] * apad_sc[SUBLANES - 2:SUBLANES - 2 + tm, :]
                + wc[1:2] * apad_sc[SUBLANES - 1:SUBLANES - 1 + tm, :]
                + wc[2:3] * a + bc_ref[:, cols])
        u = jnp.dot(hb, wl_ref[:, cols], preferred_element_type=F32)
        gated_sc[:, cols] = (jax.nn.gelu(conv) * u).astype(BF16)
    y2 = jnp.dot(gated_sc[...], wd_ref[...], preferred_element_type=F32)
    o_ref[...] = x1 + mod[5:6] * _rms(y2, g_postf_ref[...])


def _ffn_call(x2, om, od, mod, g_post, g_pre, g_postf, wo, wa, wl, wc, bc, wd, batch, seq):
    m = x2.shape[0]
    tm = TM_FFN
    per_seq = seq // tm
    row = lambda i: (i, 0)
    return pl.pallas_call(
        functools.partial(_ffn_kernel, tiles_per_seq=per_seq),
        grid=(m // tm,),
        in_specs=[
            pl.BlockSpec((tm, D_MODEL), row),
            pl.BlockSpec((tm, WIDTH), row),
            pl.BlockSpec((tm, WIDTH), row),
            pl.BlockSpec((None, SUBLANES, D_MODEL), lambda i: (i // per_seq, 0, 0)),
            _resident((1, D_MODEL)), _resident((1, D_MODEL)), _resident((1, D_MODEL)),
            _resident((D_MODEL, D_MODEL)),
            _resident((D_MODEL, D_FF)), _resident((D_MODEL, D_FF)),
            _resident((CONV_WIDTH, D_FF)), _resident((1, D_FF)),
            _resident((D_FF, D_MODEL)),
        ],
        out_specs=pl.BlockSpec((tm, D_MODEL), row),
        out_shape=jax.ShapeDtypeStruct(x2.shape, F32),
        scratch_shapes=[
            pltpu.VMEM((tm + SUBLANES, FF_CHUNK), F32),
            pltpu.VMEM((tm, D_FF), BF16),
            pltpu.VMEM((SUBLANES, D_FF), F32),
        ],
        compiler_params=pltpu.CompilerParams(
            dimension_semantics=("arbitrary",), vmem_limit_bytes=VMEM_LIMIT),
        name="ffn",
    )(x2, om, od, mod, g_post, g_pre, g_postf, wo, wa, wl, wc, bc, wd)


def kernel(x, c, w_ada, b_ada, g_pre_mix, w_in, g_moba_out, g_dsa_out, w_out, g_post_mix,
           g_pre_ffn, w_up_act, w_up_lin, w_conv, b_conv, w_down, g_post_ffn):
    batch, seq, d = x.shape
    depth = w_ada.shape[0]
    n_blk = seq // MOBA_BLOCK
    assert d == D_MODEL and w_in.shape[-1] == PROJ_WIDTH
    assert seq % TM_PROJ == 0 and seq % TM_FFN == 0 and seq % TQ == 0
    assert min(DSA_TOPK, seq // 4) == DSA_TOPK and seq < 2 ** 13

    mod = _ada_call(c, w_ada, b_ada).reshape(depth, batch, N_MOD, D_MODEL)
    mod = jnp.pad(mod, ((0, 0), (0, 0), (0, SUBLANES - N_MOD), (0, 0)))
    tab = _rope_tables(seq)
    vec = lambda g: g.reshape(1, -1)

    x2 = x.reshape(batch * seq, D_MODEL)
    for l in range(depth):
        w_in_b = jnp.pad(w_in[l].astype(BF16), ((0, 0), (0, PROJ_PAD - PROJ_WIDTH)))
        wvt = jnp.stack([w_in[l][:, 2 * WIDTH:3 * WIDTH].T, w_in[l][:, 5 * WIDTH:6 * WIDTH].T]).astype(BF16)
        mq, mk, mvt, dq, dk, dvt, qi, kib, kw, km = _proj_call(
            x2, mod[l], vec(g_pre_mix[l]), w_in_b, wvt, tab, batch, seq)
        km = km.reshape(batch, n_blk, WIDTH)
        km = jnp.pad(km, ((0, 0), (0, -n_blk % BF16_ROWS), (0, 0)))
        o_moba = _moba_call(mq, mk, mvt, km, vec(g_moba_out[l]), batch, seq)
        o_dsa = _dsa_call(dq, dk, dvt, qi, kib, kw, vec(g_dsa_out[l]), batch, seq)
        x2 = _ffn_call(
            x2, o_moba, o_dsa, mod[l], vec(g_post_mix[l]), vec(g_pre_ffn[l]), vec(g_post_ffn[l]),
            w_out[l].astype(BF16), w_up_act[l].astype(BF16), w_up_lin[l].astype(BF16),
            w_conv[l], vec(b_conv[l]), w_down[l].astype(BF16), batch, seq)
    return x2.reshape(batch, seq, D_MODEL)
```

```python
import functools
import math

import jax
import jax.numpy as jnp
from jax import lax
from jax.experimental import pallas as pl
from jax.experimental.pallas import tpu as pltpu

F32 = jnp.float32
BF16 = jnp.bfloat16

D_MODEL = 1024
HEAD_DIM = 64
N_HEADS = 8
WIDTH = N_HEADS * HEAD_DIM
ROPE_THETA = 500000.0
ROPE_DIM = HEAD_DIM // 4
MOBA_BLOCK = 256
MOBA_TOPK = 3
N_IDX_HEADS = 8
IDX_DIM = 64
DSA_TOPK = 256
D_FF = 2816
CONV_WIDTH = 3
NORM_EPS = 1e-6
N_MOD = 6
PROJ_WIDTH = 6 * WIDTH + N_IDX_HEADS * IDX_DIM + IDX_DIM + N_IDX_HEADS

LANES = 128
SUBLANES = 8
BF16_ROWS = 16
PROJ_PAD = 29 * LANES
TAIL_COL = 7 * WIDTH
IDX_SCALE = IDX_DIM ** -0.5
Q_SCALE = HEAD_DIM ** -0.5 * math.log2(math.e)
W_SCALE = N_IDX_HEADS ** -0.5
NEG_INF = float("-inf")
POS_INF = float("inf")
M_INIT = -1e30
INT_MIN = -(2 ** 31)
HALF_MIN = -(2 ** 15)

TM_PROJ = 512
TM_FFN = 512
TQ = MOBA_BLOCK
TK = MOBA_BLOCK
FF_CHUNK = 256
HEAD_LOOKAHEAD = 4
VMEM_LIMIT = 52 * 1024 * 1024

_CONTRACT_LANES = (((1,), (1,)), ((), ()))


def _rms(x, g):
    return x * lax.rsqrt(jnp.mean(x * x, axis=-1, keepdims=True) + NORM_EPS) * g


def _dot_t(a, b):
    return lax.dot_general(a, b, _CONTRACT_LANES, preferred_element_type=F32)


def _resident(shape):
    return pl.BlockSpec(shape, lambda *_: (0,) * len(shape), pipeline_mode=pl.Buffered(1))


def _ada_kernel(c_ref, w_ref, b_ref, o_ref):
    c = c_ref[...]
    c_act = (c * jax.nn.sigmoid(c)).astype(BF16)
    o_ref[...] = jnp.dot(c_act, w_ref[...].astype(BF16), preferred_element_type=F32) + b_ref[...]


def _ada_call(c, w_ada, b_ada):
    depth, d, n = w_ada.shape
    b = c.shape[0]
    tn = n // 4
    return pl.pallas_call(
        _ada_kernel,
        grid=(depth, n // tn),
        in_specs=[
            pl.BlockSpec((b, d), lambda l, j: (0, 0)),
            pl.BlockSpec((None, d, tn), lambda l, j: (l, 0, j)),
            pl.BlockSpec((None, 1, tn), lambda l, j: (l, 0, j)),
        ],
        out_specs=pl.BlockSpec((None, b, tn), lambda l, j: (l, 0, j)),
        out_shape=jax.ShapeDtypeStruct((depth, b, n), F32),
        compiler_params=pltpu.CompilerParams(vmem_limit_bytes=VMEM_LIMIT),
        name="ada",
    )(c, w_ada, b_ada.reshape(depth, 1, n))


def _rope_tables(seq):
    pos = jnp.arange(seq, dtype=F32)
    inv_freq = ROPE_THETA ** (-jnp.arange(0, ROPE_DIM, 2, dtype=F32) / ROPE_DIM)
    ang = pos[:, None] * inv_freq[None, :]
    cos, sin = jnp.cos(ang), jnp.sin(ang)
    half = ROPE_DIM // 2
    d = jnp.arange(LANES) % HEAD_DIM
    f = d % half
    t_self = jnp.where(d < ROPE_DIM, cos[:, f], 1.0)
    t_up = jnp.where(d < half, -sin[:, f], 0.0)
    t_dn = jnp.where((d >= half) & (d < ROPE_DIM), sin[:, f], 0.0)
    return jnp.concatenate([t_self, t_up, t_dn], axis=1).astype(F32)


def _proj_kernel(x_ref, mod_ref, g_ref, w_ref, wvt_ref, tab_ref,
                 mq_ref, mk_ref, mvt_ref, dq_ref, dk_ref, dvt_ref, qi_ref, kib_ref, kw_ref, km_ref):
    mod = mod_ref[...]
    h = _rms(x_ref[...], g_ref[...]) * (1.0 + mod[1:2]) + mod[0:1]
    hb = h.astype(BF16)
    tab = tab_ref[...]
    t_self, t_up, t_dn = tab[:, :LANES], tab[:, LANES:2 * LANES], tab[:, 2 * LANES:]
    half = ROPE_DIM // 2

    def rope(a):
        up = pltpu.roll(a, LANES - half, axis=1)
        dn = pltpu.roll(a, half, axis=1)
        return a * t_self + up * t_up + dn * t_dn

    def roped_group(g_idx, scale):
        a = jnp.dot(hb, w_ref[:, g_idx * WIDTH:(g_idx + 1) * WIDTH], preferred_element_type=F32)
        if scale is not None:
            a = a * scale
        return jnp.concatenate([rope(a[:, c * LANES:(c + 1) * LANES]) for c in range(WIDTH // LANES)], axis=1)

    def values_t(v_idx, out_ref):
        vt = _dot_t(wvt_ref[v_idx], hb).astype(BF16)
        for t in range(vt.shape[1] // TK):
            out_ref[t] = vt[:, t * TK:(t + 1) * TK]

    mq_ref[...] = roped_group(0, Q_SCALE).astype(BF16)
    mk = roped_group(1, None)
    mk_ref[...] = mk.astype(BF16)
    km_ref[...] = jnp.mean(mk.reshape(mk.shape[0] // MOBA_BLOCK, MOBA_BLOCK, WIDTH), axis=1)
    values_t(0, mvt_ref)
    dq_ref[...] = roped_group(3, Q_SCALE).astype(BF16)
    dk_ref[...] = roped_group(4, None).astype(BF16)
    values_t(1, dvt_ref)
    qi_ref[...] = roped_group(6, IDX_SCALE).astype(BF16)

    tail = jnp.dot(hb, w_ref[:, TAIL_COL:TAIL_COL + LANES], preferred_element_type=F32)
    lane = lax.broadcasted_iota(jnp.int32, tail.shape, 1)
    is_key = lane < IDX_DIM
    roped = rope(tail)
    kw_ref[...] = jnp.where(is_key, roped, tail)
    kib_ref[...] = jnp.where(is_key, roped, pltpu.roll(roped, IDX_DIM, axis=1)).astype(BF16)


def _proj_call(x2, mod, g_pre, w_in_b, wvt, tab, batch, seq):
    m = x2.shape[0]
    tm = TM_PROJ
    per_seq = seq // tm
    nblk = tm // MOBA_BLOCK
    row = lambda i: (i, 0)
    lead = lambda i: (i, 0, 0)
    wide = jax.ShapeDtypeStruct((m, WIDTH), BF16)
    wide_t = jax.ShapeDtypeStruct((m // TK, WIDTH, TK), BF16)
    wide_spec = pl.BlockSpec((tm, WIDTH), row)
    wide_t_spec = pl.BlockSpec((tm // TK, WIDTH, TK), lead)
    return pl.pallas_call(
        _proj_kernel,
        grid=(m // tm,),
        in_specs=[
            pl.BlockSpec((tm, D_MODEL), row),
            pl.BlockSpec((None, SUBLANES, D_MODEL), lambda i: (i // per_seq, 0, 0)),
            _resident((1, D_MODEL)),
            _resident((D_MODEL, PROJ_PAD)),
            _resident((2, WIDTH, D_MODEL)),
            pl.BlockSpec((tm, 3 * LANES), lambda i: (i % per_seq, 0)),
        ],
        out_specs=[wide_spec, wide_spec, wide_t_spec, wide_spec, wide_spec, wide_t_spec, wide_spec,
                   pl.BlockSpec((tm, LANES), row),
                   pl.BlockSpec((tm, LANES), row),
                   pl.BlockSpec((None, nblk, WIDTH), lead)],
        out_shape=[wide, wide, wide_t, wide, wide, wide_t, wide,
                   jax.ShapeDtypeStruct((m, LANES), BF16),
                   jax.ShapeDtypeStruct((m, LANES), F32),
                   jax.ShapeDtypeStruct((m // tm, nblk, WIDTH), F32)],
        compiler_params=pltpu.CompilerParams(
            dimension_semantics=("parallel",), vmem_limit_bytes=VMEM_LIMIT),
        name="proj",
    )(x2, mod, g_pre, w_in_b, wvt, tab)


def _mask_heads(q_ref, dst_sc):
    lane = lax.broadcasted_iota(jnp.int32, (TQ, LANES), 1)
    for h in range(N_HEADS):
        c, hh = divmod(h, 2)
        pair = q_ref[:, c * LANES:(c + 1) * LANES]
        keep = (lane < HEAD_DIM) if hh == 0 else (lane >= HEAD_DIM)
        dst_sc[h] = jnp.where(keep, pair, jnp.zeros_like(pair))


def _attend_tile(score_fn, cap_fn, kv_idx, n_sub, vt_ref, m_sc, acc_sc):
    ones_rows = jnp.ones((BF16_ROWS, n_sub * TK), BF16)
    scores = {h: score_fn(h) for h in range(min(HEAD_LOOKAHEAD, N_HEADS))}
    for h in range(N_HEADS):
        s = scores.pop(h)
        subs = [s[t * TK:(t + 1) * TK] for t in range(n_sub)]
        caps = [cap_fn(h, t, jnp.max(subs[t], axis=0, keepdims=True).astype(F32)) for t in range(n_sub)]
        m_old = m_sc[h]
        m_new = functools.reduce(jnp.maximum, caps, m_old)
        p = jnp.concatenate(
            [jnp.exp2(subs[t] - jnp.where(caps[t] > NEG_INF, m_new, POS_INF).astype(BF16)) for t in range(n_sub)],
            axis=0)
        alpha = jnp.exp2(m_old - m_new)
        m_sc[h] = m_new
        if h + HEAD_LOOKAHEAD < N_HEADS:
            scores[h + HEAD_LOOKAHEAD] = score_fn(h + HEAD_LOOKAHEAD)
        values_t = jnp.concatenate(
            [vt_ref[kv_idx + t, h * HEAD_DIM:(h + 1) * HEAD_DIM, :] for t in range(n_sub)], axis=1)
        vt_tile = jnp.concatenate([values_t, ones_rows], axis=0)
        acc_sc[h] = alpha * acc_sc[h] + jnp.dot(vt_tile, p, preferred_element_type=F32)


def _over_key_tiles(n_tiles, step):
    def pair(jj, _):
        step(2 * jj, 2)
        return 0

    lax.fori_loop(0, n_tiles // 2, pair, 0)

    @pl.when(n_tiles % 2 == 1)
    def _():
        step(n_tiles - 1, 1)


def _init_stats(m_sc, acc_sc):
    for h in range(N_HEADS):
        m_sc[h] = jnp.full((1, TQ), M_INIT, F32)
        acc_sc[h] = jnp.zeros((HEAD_DIM + BF16_ROWS, TQ), F32)


def _finish(acc_sc, g_ref, o_ref):
    o_t = jnp.concatenate(
        [acc_sc[h, :HEAD_DIM, :] * (1.0 / acc_sc[h, HEAD_DIM:HEAD_DIM + 1, :]) for h in range(N_HEADS)], axis=0)
    o_t = o_t * lax.rsqrt(jnp.mean(o_t * o_t, axis=0, keepdims=True) + NORM_EPS)
    o_ref[...] = (o_t.T * g_ref[...]).astype(BF16)


def _attn_scratch():
    return [
        pltpu.VMEM((N_HEADS, TQ, LANES), BF16),
        pltpu.VMEM((N_HEADS, 1, TQ), F32),
        pltpu.VMEM((N_HEADS, HEAD_DIM + BF16_ROWS, TQ), F32),
    ]


def _key_query_iotas():
    return (lax.broadcasted_iota(jnp.int32, (TK, TQ), 0), lax.broadcasted_iota(jnp.int32, (TK, TQ), 1))


def _moba_kernel(q_ref, k_ref, vt_ref, km_ref, g_ref, o_ref, qm_sc, m_sc, acc_sc, sel_sc):
    i = pl.program_id(1)
    n_blk = km_ref.shape[0]
    _mask_heads(q_ref, qm_sc)
    _init_stats(m_sc, acc_sc)

    blk = lax.broadcasted_iota(jnp.int32, (n_blk, TQ), 0)
    blk_f = blk.astype(F32)
    for h in range(N_HEADS):
        cols = slice((h // 2) * LANES, (h // 2 + 1) * LANES)
        gate = jnp.where(blk < i, _dot_t(km_ref[:, cols].astype(BF16), qm_sc[h]), NEG_INF)
        sel = jnp.zeros((n_blk, TQ), F32)
        for _ in range(MOBA_TOPK):
            top = jnp.max(gate, axis=0, keepdims=True)
            first = jnp.min(jnp.where(gate == top, blk_f, float(n_blk)), axis=0, keepdims=True)
            is_first = blk_f == first
            sel = jnp.where(is_first & (top > NEG_INF), 1.0, sel)
            gate = jnp.where(is_first, NEG_INF, gate)
        sel_sc[h] = sel

    def raw_scores(j, n_sub):
        rows = pl.ds(pl.multiple_of(j * TK, TK), n_sub * TK)
        return lambda h: _dot_t(k_ref[rows, (h // 2) * LANES:(h // 2 + 1) * LANES], qm_sc[h]).astype(BF16)

    def past_step(j, n_sub):
        def cap(h, t, top):
            chosen = sel_sc[h, pl.ds(j + t, 1), :]
            return jnp.where(chosen > 0.0, top, NEG_INF)
        _attend_tile(raw_scores(j, n_sub), cap, j, n_sub, vt_ref, m_sc, acc_sc)

    _over_key_tiles(i, past_step)

    key_pos, query_pos = _key_query_iotas()
    own_scores = raw_scores(i, 1)
    _attend_tile(lambda h: jnp.where(key_pos <= query_pos, own_scores(h), NEG_INF),
                 lambda h, t, top: top, i, 1, vt_ref, m_sc, acc_sc)

    _finish(acc_sc, g_ref, o_ref)


def _moba_call(mq, mk, mvt, km, g, batch, seq):
    per_seq = seq // TQ
    n_blk = km.shape[1]
    q_spec = pl.BlockSpec((TQ, WIDTH), lambda b, i: (b * per_seq + i, 0))
    return pl.pallas_call(
        _moba_kernel,
        grid=(batch, per_seq),
        in_specs=[
            q_spec,
            pl.BlockSpec((seq, WIDTH), lambda b, i: (b, 0)),
            pl.BlockSpec((seq // TK, WIDTH, TK), lambda b, i: (b, 0, 0)),
            pl.BlockSpec((None, n_blk, WIDTH), lambda b, i: (b, 0, 0)),
            _resident((1, WIDTH)),
        ],
        out_specs=q_spec,
        out_shape=jax.ShapeDtypeStruct(mq.shape, BF16),
        scratch_shapes=_attn_scratch() + [
            pltpu.VMEM((N_HEADS, n_blk, TQ), F32),
        ],
        compiler_params=pltpu.CompilerParams(
            dimension_semantics=("parallel", "parallel"), vmem_limit_bytes=VMEM_LIMIT),
        name="moba",
    )(mq, mk, mvt, km, g)


def _dsa_kernel(q_ref, k_ref, vt_ref, qi_ref, kib_ref, kw_ref, g_ref, o_ref,
                qm_sc, m_sc, acc_sc, qim_sc, w_sc, key_sc, half_sc):
    i = pl.program_id(1)
    n_kv = i + 1
    key_pos, query_pos = _key_query_iotas()
    _mask_heads(q_ref, qm_sc)
    _mask_heads(qi_ref, qim_sc)
    _init_stats(m_sc, acc_sc)
    w_sc[...] = kw_ref[...].T[IDX_DIM:IDX_DIM + N_IDX_HEADS, :] * W_SCALE

    def score_tile(j):
        rows = pl.ds(pl.multiple_of(j * TK, TK), TK)
        k_idx = kib_ref[rows, :]
        score = jnp.zeros((TK, TQ), F32)
        for h in range(N_IDX_HEADS):
            score = score + w_sc[h:h + 1, :] * jnp.maximum(_dot_t(k_idx, qim_sc[h]), 0.0)
        bits = pltpu.bitcast(score + 0.0, jnp.int32)
        key = jnp.where(bits < 0, bits ^ jnp.int32(0x7FFFFFFF), bits)
        admissible = (key_pos <= query_pos) | (j < i)
        key = jnp.where(admissible, key, jnp.int32(INT_MIN))
        key_sc[j] = key
        half_sc[j] = lax.shift_right_arithmetic(key, 16).astype(jnp.int16)

    def score_step(j, n_sub):
        for t in range(n_sub):
            score_tile(j + t)

    _over_key_tiles(n_kv, score_step)

    @pl.when(n_kv % 2 == 1)
    def _():
        half_sc[n_kv] = jnp.full((TK, TQ), HALF_MIN, jnp.int16)

    def count(pred):
        def step(j, acc):
            return acc + jnp.sum(jnp.where(pred(key_sc[j], j), 1.0, 0.0), axis=0, keepdims=True)
        return lax.fori_loop(0, n_kv, step, jnp.zeros((1, TQ), F32))

    def count_half(pred):
        def step(jj, acc):
            for t in range(2):
                ones = jnp.where(pred(half_sc[2 * jj + t]), jnp.int16(1), jnp.int16(0))
                for r in range(TK // BF16_ROWS):
                    acc = acc + ones[r * BF16_ROWS:(r + 1) * BF16_ROWS]
            return acc
        acc = lax.fori_loop(0, (n_kv + 1) // 2, step, jnp.zeros((BF16_ROWS, TQ), jnp.int16))
        return jnp.sum(acc.astype(jnp.int32), axis=0, keepdims=True)

    def bisect_half(need):
        def bit_step(b, thr):
            cand = thr + lax.shift_left(jnp.int32(1), 15 - b)
            cand16 = cand.astype(jnp.int16)
            cnt = count_half(lambda half: half >= cand16)
            return jnp.where(cnt >= need, cand, thr)
        return lax.fori_loop(0, 16, bit_step, jnp.full((1, TQ), HALF_MIN, jnp.int32))

    hi = bisect_half(jnp.full((1, TQ), DSA_TOPK, jnp.int32))
    hi16 = hi.astype(jnp.int16)
    need_lo = DSA_TOPK - count_half(lambda half: half > hi16)

    def to_lower_half(j, _):
        lower = ((key_sc[j] & jnp.int32(0xFFFF)) + jnp.int32(HALF_MIN)).astype(jnp.int16)
        half_sc[j] = jnp.where(half_sc[j] == hi16, lower, jnp.int16(HALF_MIN))
        return 0

    lax.fori_loop(0, n_kv, to_lower_half, 0)
    lo = bisect_half(need_lo)
    thr = hi * jnp.int32(2 ** 16) + (lo - jnp.int32(HALF_MIN))

    n_ge = count(lambda k, j: k >= thr)
    excess = (n_ge > float(DSA_TOPK)) & (thr > jnp.int32(INT_MIN))

    @pl.when(jnp.max(jnp.where(excess, 1.0, 0.0)) > 0.0)
    def _():
        need = float(DSA_TOPK) - count(lambda k, j: k > thr)

        def idx_step(b, cut):
            cand = cut + lax.shift_left(jnp.int32(1), 12 - b)
            cnt = count(lambda k, j: (k == thr) & (j * TK + key_pos < cand))
            return jnp.where(cnt <= need, cand, cut)

        cut = lax.fori_loop(0, 13, idx_step, jnp.zeros((1, TQ), jnp.int32))

        def demote(j, _):
            k = key_sc[j]
            drop = excess & (k == thr) & (j * TK + key_pos >= cut)
            key_sc[j] = jnp.where(drop, thr - 1, k)
            return 0

        lax.fori_loop(0, n_kv, demote, 0)

    thr = jnp.maximum(thr, jnp.int32(INT_MIN + 1))

    def attn_step(j, n_sub):
        rows = pl.ds(pl.multiple_of(j * TK, TK), n_sub * TK)
        bias = jnp.concatenate(
            [jnp.where(key_sc[j + t] >= thr, 0.0, NEG_INF) for t in range(n_sub)], axis=0).astype(BF16)
        _attend_tile(
            lambda h: _dot_t(k_ref[rows, (h // 2) * LANES:(h // 2 + 1) * LANES], qm_sc[h]).astype(BF16) + bias,
            lambda h, t, top: top, j, n_sub, vt_ref, m_sc, acc_sc)

    _over_key_tiles(n_kv, attn_step)
    _finish(acc_sc, g_ref, o_ref)


def _dsa_call(dq, dk, dvt, qi, kib, kw, g, batch, seq):
    per_seq = seq // TQ
    q_spec = pl.BlockSpec((TQ, WIDTH), lambda b, i: (b * per_seq + i, 0))
    return pl.pallas_call(
        _dsa_kernel,
        grid=(batch, per_seq),
        in_specs=[
            q_spec,
            pl.BlockSpec((seq, WIDTH), lambda b, i: (b, 0)),
            pl.BlockSpec((seq // TK, WIDTH, TK), lambda b, i: (b, 0, 0)),
            q_spec,
            pl.BlockSpec((seq, LANES), lambda b, i: (b, 0)),
            pl.BlockSpec((TQ, LANES), lambda b, i: (b * per_seq + i, 0)),
            _resident((1, WIDTH)),
        ],
        out_specs=q_spec,
        out_shape=jax.ShapeDtypeStruct(dq.shape, BF16),
        scratch_shapes=_attn_scratch() + [
            pltpu.VMEM((N_IDX_HEADS, TQ, LANES), BF16),
            pltpu.VMEM((N_IDX_HEADS, TQ), F32),
            pltpu.VMEM((per_seq, TK, TQ), jnp.int32),
            pltpu.VMEM((per_seq, TK, TQ), jnp.int16),
        ],
        compiler_params=pltpu.CompilerParams(
            dimension_semantics=("parallel", "parallel"), vmem_limit_bytes=VMEM_LIMIT),
        name="dsa",
    )(dq, dk, dvt, qi, kib, kw, g)


def _ffn_kernel(x_ref, om_ref, od_ref, mod_ref, g_post_ref, g_pre_ref, g_postf_ref,
                wo_ref, wa_ref, wl_ref, wc_ref, bc_ref, wd_ref, o_ref,
                apad_sc, gated_sc, tail_sc, *, tiles_per_seq):
    tm = x_ref.shape[0]
    first = (pl.program_id(0) % tiles_per_seq) == 0
    mod = mod_ref[...]
    o = jnp.concatenate([om_ref[...], od_ref[...]], axis=1)
    y = jnp.dot(o, wo_ref[...], preferred_element_type=F32)
    x1 = x_ref[...] + mod[2:3] * _rms(y, g_post_ref[...])
    hb = (_rms(x1, g_pre_ref[...]) * (1.0 + mod[4:5]) + mod[3:4]).astype(BF16)
    for c in range(D_FF // FF_CHUNK):
        cols = slice(c * FF_CHUNK, (c + 1) * FF_CHUNK)
        a = jnp.dot(hb, wa_ref[:, cols], preferred_element_type=F32)
        apad_sc[0:SUBLANES, :] = jnp.where(first, 0.0, tail_sc[:, cols])
        apad_sc[SUBLANES:, :] = a
        tail_sc[:, cols] = a[tm - SUBLANES:, :]
        wc = wc_ref[:, cols]
        conv = (wc[0:1] * apad_sc[SUBLANES - 2:SUBLANES - 2 + tm, :]
                + wc[1:2] * apad_sc[SUBLANES - 1:SUBLANES - 1 + tm, :]
                + wc[2:3] * a + bc_ref[:, cols])
        u = jnp.dot(hb, wl_ref[:, cols], preferred_element_type=F32)
        gated_sc[:, cols] = (jax.nn.gelu(conv) * u).astype(BF16)
    y2 = jnp.dot(gated_sc[...], wd_ref[...], preferred_element_type=F32)
    o_ref[...] = x1 + mod[5:6] * _rms(y2, g_postf_ref[...])


def _ffn_call(x2, om, od, mod, g_post, g_pre, g_postf, wo, wa, wl, wc, bc, wd, batch, seq):
    m = x2.shape[0]
    tm = TM_FFN
    per_seq = seq // tm
    row = lambda i: (i, 0)
    return pl.pallas_call(
        functools.partial(_ffn_kernel, tiles_per_seq=per_seq),
        grid=(m // tm,),
        in_specs=[
            pl.BlockSpec((tm, D_MODEL), row),
            pl.BlockSpec((tm, WIDTH), row),
            pl.BlockSpec((tm, WIDTH), row),
            pl.BlockSpec((None, SUBLANES, D_MODEL), lambda i: (i // per_seq, 0, 0)),
            _resident((1, D_MODEL)), _resident((1, D_MODEL)), _resident((1, D_MODEL)),
            _resident((D_MODEL, D_MODEL)),
            _resident((D_MODEL, D_FF)), _resident((D_MODEL, D_FF)),
            _resident((CONV_WIDTH, D_FF)), _resident((1, D_FF)),
            _resident((D_FF, D_MODEL)),
        ],
        out_specs=pl.BlockSpec((tm, D_MODEL), row),
        out_shape=jax.ShapeDtypeStruct(x2.shape, F32),
        scratch_shapes=[
            pltpu.VMEM((tm + SUBLANES, FF_CHUNK), F32),
            pltpu.VMEM((tm, D_FF), BF16),
            pltpu.VMEM((SUBLANES, D_FF), F32),
        ],
        compiler_params=pltpu.CompilerParams(
            dimension_semantics=("arbitrary",), vmem_limit_bytes=VMEM_LIMIT),
        name="ffn",
    )(x2, om, od, mod, g_post, g_pre, g_postf, wo, wa, wl, wc, bc, wd)


def kernel(x, c, w_ada, b_ada, g_pre_mix, w_in, g_moba_out, g_dsa_out, w_out, g_post_mix,
           g_pre_ffn, w_up_act, w_up_lin, w_conv, b_conv, w_down, g_post_ffn):
    batch, seq, d = x.shape
    depth = w_ada.shape[0]
    n_blk = seq // MOBA_BLOCK
    assert d == D_MODEL and w_in.shape[-1] == PROJ_WIDTH
    assert seq % TM_PROJ == 0 and seq % TM_FFN == 0 and seq % (2 * TK) == 0 and TQ == TK
    assert min(DSA_TOPK, seq // 4) == DSA_TOPK and seq < 2 ** 13

    mod = _ada_call(c, w_ada, b_ada).reshape(depth, batch, N_MOD, D_MODEL)
    mod = jnp.pad(mod, ((0, 0), (0, 0), (0, SUBLANES - N_MOD), (0, 0)))
    tab = _rope_tables(seq)
    vec = lambda g: g.reshape(1, -1)

    x2 = x.reshape(batch * seq, D_MODEL)
    for l in range(depth):
        w_in_b = jnp.pad(w_in[l].astype(BF16), ((0, 0), (0, PROJ_PAD - PROJ_WIDTH)))
        wvt = jnp.stack([w_in[l][:, 2 * WIDTH:3 * WIDTH].T, w_in[l][:, 5 * WIDTH:6 * WIDTH].T]).astype(BF16)
        mq, mk, mvt, dq, dk, dvt, qi, kib, kw, km = _proj_call(
            x2, mod[l], vec(g_pre_mix[l]), w_in_b, wvt, tab, batch, seq)
        km = km.reshape(batch, n_blk, WIDTH)
        km = jnp.pad(km, ((0, 0), (0, -n_blk % BF16_ROWS), (0, 0)))
        o_moba = _moba_call(mq, mk, mvt, km, vec(g_moba_out[l]), batch, seq)
        o_dsa = _dsa_call(dq, dk, dvt, qi, kib, kw, vec(g_dsa_out[l]), batch, seq)
        x2 = _ffn_call(
            x2, o_moba, o_dsa, mod[l], vec(g_post_mix[l]), vec(g_pre_ffn[l]), vec(g_post_ffn[l]),
            w_out[l].astype(BF16), w_up_act[l].astype(BF16), w_up_lin[l].astype(BF16),
            w_conv[l], vec(b_conv[l]), w_down[l].astype(BF16), batch, seq)
    return x2.reshape(batch, seq, D_MODEL)
```

```python
import functools
import math

import jax
import jax.numpy as jnp
from jax import lax
from jax.experimental import pallas as pl
from jax.experimental.pallas import tpu as pltpu

F32 = jnp.float32
BF16 = jnp.bfloat16

D_MODEL = 1024
HEAD_DIM = 64
N_HEADS = 8
WIDTH = N_HEADS * HEAD_DIM
ROPE_THETA = 500000.0
ROPE_DIM = HEAD_DIM // 4
MOBA_BLOCK = 256
MOBA_TOPK = 3
N_IDX_HEADS = 8
IDX_DIM = 64
DSA_TOPK = 256
D_FF = 2816
CONV_WIDTH = 3
NORM_EPS = 1e-6
N_MOD = 6
PROJ_WIDTH = 6 * WIDTH + N_IDX_HEADS * IDX_DIM + IDX_DIM + N_IDX_HEADS

LANES = 128
SUBLANES = 8
BF16_ROWS = 16
PROJ_PAD = 29 * LANES
TAIL_COL = 7 * WIDTH
IDX_SCALE = IDX_DIM ** -0.5
Q_SCALE = HEAD_DIM ** -0.5 * math.log2(math.e)
W_SCALE = N_IDX_HEADS ** -0.5
NEG_INF = float("-inf")
POS_INF = float("inf")
M_INIT = -1e30
INT_MIN = -(2 ** 31)
HALF_MIN = -(2 ** 15)

TM_PROJ = 512
TM_FFN = 512
TQ = MOBA_BLOCK
TK = MOBA_BLOCK
FF_CHUNK = 256
HEAD_LOOKAHEAD = 4
VMEM_LIMIT = 52 * 1024 * 1024

_CONTRACT_LANES = (((1,), (1,)), ((), ()))


def _rms(x, g):
    return x * lax.rsqrt(jnp.mean(x * x, axis=-1, keepdims=True) + NORM_EPS) * g


def _dot_t(a, b):
    return lax.dot_general(a, b, _CONTRACT_LANES, preferred_element_type=F32)


def _dot(a, b):
    return jnp.dot(a, b, preferred_element_type=F32)


def _resident(shape):
    return pl.BlockSpec(shape, lambda *_: (0,) * len(shape), pipeline_mode=pl.Buffered(1))


def _ada_kernel(c_ref, w_ref, b_ref, o_ref):
    c = c_ref[...]
    c_act = (c * jax.nn.sigmoid(c)).astype(BF16)
    o_ref[...] = jnp.dot(c_act, w_ref[...].astype(BF16), preferred_element_type=F32) + b_ref[...]


def _ada_call(c, w_ada, b_ada):
    depth, d, n = w_ada.shape
    b = c.shape[0]
    tn = n // 4
    return pl.pallas_call(
        _ada_kernel,
        grid=(depth, n // tn),
        in_specs=[
            pl.BlockSpec((b, d), lambda l, j: (0, 0)),
            pl.BlockSpec((None, d, tn), lambda l, j: (l, 0, j)),
            pl.BlockSpec((None, 1, tn), lambda l, j: (l, 0, j)),
        ],
        out_specs=pl.BlockSpec((None, b, tn), lambda l, j: (l, 0, j)),
        out_shape=jax.ShapeDtypeStruct((depth, b, n), F32),
        compiler_params=pltpu.CompilerParams(vmem_limit_bytes=VMEM_LIMIT),
        name="ada",
    )(c, w_ada, b_ada.reshape(depth, 1, n))


def _rope_tables(seq):
    pos = jnp.arange(seq, dtype=F32)
    inv_freq = ROPE_THETA ** (-jnp.arange(0, ROPE_DIM, 2, dtype=F32) / ROPE_DIM)
    ang = pos[:, None] * inv_freq[None, :]
    cos, sin = jnp.cos(ang), jnp.sin(ang)
    half = ROPE_DIM // 2
    d = jnp.arange(LANES) % HEAD_DIM
    f = d % half
    t_self = jnp.where(d < ROPE_DIM, cos[:, f], 1.0)
    t_up = jnp.where(d < half, -sin[:, f], 0.0)
    t_dn = jnp.where((d >= half) & (d < ROPE_DIM), sin[:, f], 0.0)
    return jnp.concatenate([t_self, t_up, t_dn], axis=1).astype(F32)


def _proj_kernel(x_ref, mod_ref, g_ref, w_ref, wvt_ref, tab_ref,
                 mq_ref, mk_ref, mvt_ref, dq_ref, dk_ref, dvt_ref, qi_ref, kib_ref, kw_ref, km_ref):
    mod = mod_ref[...]
    h = _rms(x_ref[...], g_ref[...]) * (1.0 + mod[1:2]) + mod[0:1]
    hb = h.astype(BF16)
    tab = tab_ref[...]
    t_self, t_up, t_dn = tab[:, :LANES], tab[:, LANES:2 * LANES], tab[:, 2 * LANES:]
    half = ROPE_DIM // 2

    def rope(a):
        up = pltpu.roll(a, LANES - half, axis=1)
        dn = pltpu.roll(a, half, axis=1)
        return a * t_self + up * t_up + dn * t_dn

    def roped_group(g_idx, scale):
        a = jnp.dot(hb, w_ref[:, g_idx * WIDTH:(g_idx + 1) * WIDTH], preferred_element_type=F32)
        if scale is not None:
            a = a * scale
        return jnp.concatenate([rope(a[:, c * LANES:(c + 1) * LANES]) for c in range(WIDTH // LANES)], axis=1)

    def values_t(v_idx, out_ref):
        vt = _dot_t(wvt_ref[v_idx], hb).astype(BF16)
        for t in range(vt.shape[1] // TK):
            out_ref[t] = vt[:, t * TK:(t + 1) * TK]

    mq_ref[...] = roped_group(0, Q_SCALE).astype(BF16)
    mk = roped_group(1, None)
    mk_ref[...] = mk.astype(BF16)
    km_ref[...] = jnp.mean(mk.reshape(mk.shape[0] // MOBA_BLOCK, MOBA_BLOCK, WIDTH), axis=1)
    values_t(0, mvt_ref)
    dq_ref[...] = roped_group(3, Q_SCALE).astype(BF16)
    dk_ref[...] = roped_group(4, None).astype(BF16)
    values_t(1, dvt_ref)
    qi_ref[...] = roped_group(6, IDX_SCALE).astype(BF16)

    tail = jnp.dot(hb, w_ref[:, TAIL_COL:TAIL_COL + LANES], preferred_element_type=F32)
    lane = lax.broadcasted_iota(jnp.int32, tail.shape, 1)
    is_key = lane < IDX_DIM
    roped = rope(tail)
    kw_ref[...] = jnp.where(is_key, roped, tail)
    kib_ref[...] = jnp.where(is_key, roped, pltpu.roll(roped, IDX_DIM, axis=1)).astype(BF16)


def _proj_call(x2, mod, g_pre, w_in_b, wvt, tab, batch, seq):
    m = x2.shape[0]
    tm = TM_PROJ
    per_seq = seq // tm
    nblk = tm // MOBA_BLOCK
    row = lambda i: (i, 0)
    lead = lambda i: (i, 0, 0)
    wide = jax.ShapeDtypeStruct((m, WIDTH), BF16)
    wide_t = jax.ShapeDtypeStruct((m // TK, WIDTH, TK), BF16)
    wide_spec = pl.BlockSpec((tm, WIDTH), row)
    wide_t_spec = pl.BlockSpec((tm // TK, WIDTH, TK), lead)
    return pl.pallas_call(
        _proj_kernel,
        grid=(m // tm,),
        in_specs=[
            pl.BlockSpec((tm, D_MODEL), row),
            pl.BlockSpec((None, SUBLANES, D_MODEL), lambda i: (i // per_seq, 0, 0)),
            _resident((1, D_MODEL)),
            _resident((D_MODEL, PROJ_PAD)),
            _resident((2, WIDTH, D_MODEL)),
            pl.BlockSpec((tm, 3 * LANES), lambda i: (i % per_seq, 0)),
        ],
        out_specs=[wide_spec, wide_spec, wide_t_spec, wide_spec, wide_spec, wide_t_spec, wide_spec,
                   pl.BlockSpec((tm, LANES), row),
                   pl.BlockSpec((tm, LANES), row),
                   pl.BlockSpec((None, nblk, WIDTH), lead)],
        out_shape=[wide, wide, wide_t, wide, wide, wide_t, wide,
                   jax.ShapeDtypeStruct((m, LANES), BF16),
                   jax.ShapeDtypeStruct((m, LANES), F32),
                   jax.ShapeDtypeStruct((m // tm, nblk, WIDTH), F32)],
        compiler_params=pltpu.CompilerParams(
            dimension_semantics=("parallel",), vmem_limit_bytes=VMEM_LIMIT),
        name="proj",
    )(x2, mod, g_pre, w_in_b, wvt, tab)


def _mask_heads(q_ref, dst_sc):
    lane = lax.broadcasted_iota(jnp.int32, (TQ, LANES), 1)
    for h in range(N_HEADS):
        c, hh = divmod(h, 2)
        pair = q_ref[:, c * LANES:(c + 1) * LANES]
        keep = (lane < HEAD_DIM) if hh == 0 else (lane >= HEAD_DIM)
        dst_sc[h] = jnp.where(keep, pair, jnp.zeros_like(pair)).astype(F32).T.astype(BF16)


def _attend_tile(score_fn, cap_fn, kv_idx, n_sub, vt_ref, m_sc, acc_sc):
    ones_rows = jnp.ones((BF16_ROWS, n_sub * TK), BF16)
    scores = {h: score_fn(h) for h in range(min(HEAD_LOOKAHEAD, N_HEADS))}
    for h in range(N_HEADS):
        s = scores.pop(h)
        subs = [s[t * TK:(t + 1) * TK] for t in range(n_sub)]
        caps = [cap_fn(h, t, jnp.max(subs[t], axis=0, keepdims=True).astype(F32)) for t in range(n_sub)]
        m_old = m_sc[h]
        m_new = functools.reduce(jnp.maximum, caps, m_old)
        p = jnp.concatenate(
            [jnp.exp2(subs[t] - jnp.where(caps[t] > NEG_INF, m_new, POS_INF).astype(BF16)) for t in range(n_sub)],
            axis=0)
        alpha = jnp.exp2(m_old - m_new)
        m_sc[h] = m_new
        if h + HEAD_LOOKAHEAD < N_HEADS:
            scores[h + HEAD_LOOKAHEAD] = score_fn(h + HEAD_LOOKAHEAD)
        values_t = jnp.concatenate(
            [vt_ref[kv_idx + t, h * HEAD_DIM:(h + 1) * HEAD_DIM, :] for t in range(n_sub)], axis=1)
        vt_tile = jnp.concatenate([values_t, ones_rows], axis=0)
        acc_sc[h] = alpha * acc_sc[h] + jnp.dot(vt_tile, p, preferred_element_type=F32)


def _over_key_tiles(n_tiles, step):
    def pair(jj, _):
        step(2 * jj, 2)
        return 0

    lax.fori_loop(0, n_tiles // 2, pair, 0)

    @pl.when(n_tiles % 2 == 1)
    def _():
        step(n_tiles - 1, 1)


def _init_stats(m_sc, acc_sc):
    for h in range(N_HEADS):
        m_sc[h] = jnp.full((1, TQ), M_INIT, F32)
        acc_sc[h] = jnp.zeros((HEAD_DIM + BF16_ROWS, TQ), F32)


def _finish(acc_sc, g_ref, o_ref):
    o_t = jnp.concatenate(
        [acc_sc[h, :HEAD_DIM, :] * (1.0 / acc_sc[h, HEAD_DIM:HEAD_DIM + 1, :]) for h in range(N_HEADS)], axis=0)
    o_t = o_t * lax.rsqrt(jnp.mean(o_t * o_t, axis=0, keepdims=True) + NORM_EPS)
    o_ref[...] = (o_t.T * g_ref[...]).astype(BF16)


def _attn_scratch():
    return [
        pltpu.VMEM((N_HEADS, LANES, TQ), BF16),
        pltpu.VMEM((N_HEADS, 1, TQ), F32),
        pltpu.VMEM((N_HEADS, HEAD_DIM + BF16_ROWS, TQ), F32),
    ]


def _key_query_iotas():
    return (lax.broadcasted_iota(jnp.int32, (TK, TQ), 0), lax.broadcasted_iota(jnp.int32, (TK, TQ), 1))


def _moba_kernel(q_ref, k_ref, vt_ref, km_ref, g_ref, o_ref, qm_sc, m_sc, acc_sc, sel_sc):
    i = pl.program_id(1)
    n_blk = km_ref.shape[0]
    _mask_heads(q_ref, qm_sc)
    _init_stats(m_sc, acc_sc)

    blk = lax.broadcasted_iota(jnp.int32, (n_blk, TQ), 0)
    blk_f = blk.astype(F32)
    for h in range(N_HEADS):
        cols = slice((h // 2) * LANES, (h // 2 + 1) * LANES)
        gate = jnp.where(blk < i, _dot(km_ref[:, cols].astype(BF16), qm_sc[h]), NEG_INF)
        sel = jnp.zeros((n_blk, TQ), F32)
        for _ in range(MOBA_TOPK):
            top = jnp.max(gate, axis=0, keepdims=True)
            first = jnp.min(jnp.where(gate == top, blk_f, float(n_blk)), axis=0, keepdims=True)
            is_first = blk_f == first
            sel = jnp.where(is_first & (top > NEG_INF), 1.0, sel)
            gate = jnp.where(is_first, NEG_INF, gate)
        sel_sc[h] = sel

    def raw_scores(j, n_sub):
        rows = pl.ds(pl.multiple_of(j * TK, TK), n_sub * TK)
        return lambda h: _dot(k_ref[rows, (h // 2) * LANES:(h // 2 + 1) * LANES], qm_sc[h]).astype(BF16)

    def past_step(j, n_sub):
        def cap(h, t, top):
            chosen = sel_sc[h, pl.ds(j + t, 1), :]
            return jnp.where(chosen > 0.0, top, NEG_INF)
        _attend_tile(raw_scores(j, n_sub), cap, j, n_sub, vt_ref, m_sc, acc_sc)

    _over_key_tiles(i, past_step)

    key_pos, query_pos = _key_query_iotas()
    own_scores = raw_scores(i, 1)
    _attend_tile(lambda h: jnp.where(key_pos <= query_pos, own_scores(h), NEG_INF),
                 lambda h, t, top: top, i, 1, vt_ref, m_sc, acc_sc)

    _finish(acc_sc, g_ref, o_ref)


def _moba_call(mq, mk, mvt, km, g, batch, seq):
    per_seq = seq // TQ
    n_blk = km.shape[1]
    q_spec = pl.BlockSpec((TQ, WIDTH), lambda b, i: (b * per_seq + i, 0))
    return pl.pallas_call(
        _moba_kernel,
        grid=(batch, per_seq),
        in_specs=[
            q_spec,
            pl.BlockSpec((seq, WIDTH), lambda b, i: (b, 0)),
            pl.BlockSpec((seq // TK, WIDTH, TK), lambda b, i: (b, 0, 0)),
            pl.BlockSpec((None, n_blk, WIDTH), lambda b, i: (b, 0, 0)),
            _resident((1, WIDTH)),
        ],
        out_specs=q_spec,
        out_shape=jax.ShapeDtypeStruct(mq.shape, BF16),
        scratch_shapes=_attn_scratch() + [
            pltpu.VMEM((N_HEADS, n_blk, TQ), F32),
        ],
        compiler_params=pltpu.CompilerParams(
            dimension_semantics=("parallel", "parallel"), vmem_limit_bytes=VMEM_LIMIT),
        name="moba",
    )(mq, mk, mvt, km, g)


def _dsa_kernel(q_ref, k_ref, vt_ref, qi_ref, kib_ref, kw_ref, g_ref, o_ref,
                qm_sc, m_sc, acc_sc, qim_sc, w_sc, key_sc, half_sc):
    i = pl.program_id(1)
    n_kv = i + 1
    key_pos, query_pos = _key_query_iotas()
    _mask_heads(q_ref, qm_sc)
    _mask_heads(qi_ref, qim_sc)
    _init_stats(m_sc, acc_sc)
    w_sc[...] = kw_ref[...].T[IDX_DIM:IDX_DIM + N_IDX_HEADS, :] * W_SCALE

    def score_tile(j):
        rows = pl.ds(pl.multiple_of(j * TK, TK), TK)
        k_idx = kib_ref[rows, :]
        score = jnp.zeros((TK, TQ), F32)
        for h in range(N_IDX_HEADS):
            score = score + w_sc[h:h + 1, :] * jnp.maximum(_dot(k_idx, qim_sc[h]), 0.0)
        bits = pltpu.bitcast(score + 0.0, jnp.int32)
        key = jnp.where(bits < 0, bits ^ jnp.int32(0x7FFFFFFF), bits)
        admissible = (key_pos <= query_pos) | (j < i)
        key = jnp.where(admissible, key, jnp.int32(INT_MIN))
        key_sc[j] = key
        half_sc[j] = lax.shift_right_arithmetic(key, 16).astype(jnp.int16)

    def score_step(j, n_sub):
        for t in range(n_sub):
            score_tile(j + t)

    _over_key_tiles(n_kv, score_step)

    @pl.when(n_kv % 2 == 1)
    def _():
        half_sc[n_kv] = jnp.full((TK, TQ), HALF_MIN, jnp.int16)

    def count(pred):
        def step(j, acc):
            return acc + jnp.sum(jnp.where(pred(key_sc[j], j), 1.0, 0.0), axis=0, keepdims=True)
        return lax.fori_loop(0, n_kv, step, jnp.zeros((1, TQ), F32))

    def count_half(pred):
        def step(jj, acc):
            slabs = []
            for t in range(2):
                ones = jnp.where(pred(half_sc[2 * jj + t]), jnp.int16(1), jnp.int16(0))
                slabs += [ones[r * BF16_ROWS:(r + 1) * BF16_ROWS] for r in range(TK // BF16_ROWS)]
            while len(slabs) > 1:
                slabs = [a + b for a, b in zip(slabs[::2], slabs[1::2])]
            return acc + slabs[0]
        acc = lax.fori_loop(0, (n_kv + 1) // 2, step, jnp.zeros((BF16_ROWS, TQ), jnp.int16))
        return jnp.sum(acc.astype(jnp.int32), axis=0, keepdims=True)

    def bisect_half(need):
        def bit_step(b, thr):
            cand = thr + lax.shift_left(jnp.int32(1), 15 - b)
            cand16 = cand.astype(jnp.int16)
            cnt = count_half(lambda half: half >= cand16)
            return jnp.where(cnt >= need, cand, thr)
        return lax.fori_loop(0, 16, bit_step, jnp.full((1, TQ), HALF_MIN, jnp.int32))

    hi = bisect_half(jnp.full((1, TQ), DSA_TOPK, jnp.int32))
    hi16 = hi.astype(jnp.int16)
    need_lo = DSA_TOPK - count_half(lambda half: half > hi16)

    def to_lower_half(j, _):
        lower = ((key_sc[j] & jnp.int32(0xFFFF)) + jnp.int32(HALF_MIN)).astype(jnp.int16)
        half_sc[j] = jnp.where(half_sc[j] == hi16, lower, jnp.int16(HALF_MIN))
        return 0

    lax.fori_loop(0, n_kv, to_lower_half, 0)
    lo = bisect_half(need_lo)
    thr = hi * jnp.int32(2 ** 16) + (lo - jnp.int32(HALF_MIN))

    n_ge = count(lambda k, j: k >= thr)
    excess = (n_ge > float(DSA_TOPK)) & (thr > jnp.int32(INT_MIN))

    @pl.when(jnp.max(jnp.where(excess, 1.0, 0.0)) > 0.0)
    def _():
        need = float(DSA_TOPK) - count(lambda k, j: k > thr)

        def idx_step(b, cut):
            cand = cut + lax.shift_left(jnp.int32(1), 12 - b)
            cnt = count(lambda k, j: (k == thr) & (j * TK + key_pos < cand))
            return jnp.where(cnt <= need, cand, cut)

        cut = lax.fori_loop(0, 13, idx_step, jnp.zeros((1, TQ), jnp.int32))

        def demote(j, _):
            k = key_sc[j]
            drop = excess & (k == thr) & (j * TK + key_pos >= cut)
            key_sc[j] = jnp.where(drop, thr - 1, k)
            return 0

        lax.fori_loop(0, n_kv, demote, 0)

    thr = jnp.maximum(thr, jnp.int32(INT_MIN + 1))

    def attn_step(j, n_sub):
        rows = pl.ds(pl.multiple_of(j * TK, TK), n_sub * TK)
        bias = jnp.concatenate(
            [jnp.where(key_sc[j + t] >= thr, 0.0, NEG_INF) for t in range(n_sub)], axis=0).astype(BF16)
        _attend_tile(
            lambda h: _dot(k_ref[rows, (h // 2) * LANES:(h // 2 + 1) * LANES], qm_sc[h]).astype(BF16) + bias,
            lambda h, t, top: top, j, n_sub, vt_ref, m_sc, acc_sc)

    _over_key_tiles(n_kv, attn_step)
    _finish(acc_sc, g_ref, o_ref)


def _dsa_call(dq, dk, dvt, qi, kib, kw, g, batch, seq):
    per_seq = seq // TQ
    q_spec = pl.BlockSpec((TQ, WIDTH), lambda b, i: (b * per_seq + i, 0))
    return pl.pallas_call(
        _dsa_kernel,
        grid=(batch, per_seq),
        in_specs=[
            q_spec,
            pl.BlockSpec((seq, WIDTH), lambda b, i: (b, 0)),
            pl.BlockSpec((seq // TK, WIDTH, TK), lambda b, i: (b, 0, 0)),
            q_spec,
            pl.BlockSpec((seq, LANES), lambda b, i: (b, 0)),
            pl.BlockSpec((TQ, LANES), lambda b, i: (b * per_seq + i, 0)),
            _resident((1, WIDTH)),
        ],
        out_specs=q_spec,
        out_shape=jax.ShapeDtypeStruct(dq.shape, BF16),
        scratch_shapes=_attn_scratch() + [
            pltpu.VMEM((N_IDX_HEADS, LANES, TQ), BF16),
            pltpu.VMEM((N_IDX_HEADS, TQ), F32),
            pltpu.VMEM((per_seq, TK, TQ), jnp.int32),
            pltpu.VMEM((per_seq, TK, TQ), jnp.int16),
        ],
        compiler_params=pltpu.CompilerParams(
            dimension_semantics=("parallel", "parallel"), vmem_limit_bytes=VMEM_LIMIT),
        name="dsa",
    )(dq, dk, dvt, qi, kib, kw, g)


def _ffn_kernel(x_ref, om_ref, od_ref, mod_ref, g_post_ref, g_pre_ref, g_postf_ref,
                wo_ref, wa_ref, wl_ref, wc_ref, bc_ref, wd_ref, o_ref,
                apad_sc, gated_sc, tail_sc, *, tiles_per_seq):
    tm = x_ref.shape[0]
    first = (pl.program_id(0) % tiles_per_seq) == 0
    mod = mod_ref[...]
    o = jnp.concatenate([om_ref[...], od_ref[...]], axis=1)
    y = jnp.dot(o, wo_ref[...], preferred_element_type=F32)
    x1 = x_ref[...] + mod[2:3] * _rms(y, g_post_ref[...])
    hb = (_rms(x1, g_pre_ref[...]) * (1.0 + mod[4:5]) + mod[3:4]).astype(BF16)
    for c in range(D_FF // FF_CHUNK):
        cols = slice(c * FF_CHUNK, (c + 1) * FF_CHUNK)
        a = jnp.dot(hb, wa_ref[:, cols], preferred_element_type=F32)
        apad_sc[0:SUBLANES, :] = jnp.where(first, 0.0, tail_sc[:, cols])
        apad_sc[SUBLANES:, :] = a
        tail_sc[:, cols] = a[tm - SUBLANES:, :]
        wc = wc_ref[:, cols]
        conv = (wc[0:1] * apad_sc[SUBLANES - 2:SUBLANES - 2 + tm, :]
                + wc[1:2] * apad_sc[SUBLANES - 1:SUBLANES - 1 + tm, :]
                + wc[2:3] * a + bc_ref[:, cols])
        u = jnp.dot(hb, wl_ref[:, cols], preferred_element_type=F32)
        gated_sc[:, cols] = (jax.nn.gelu(conv) * u).astype(BF16)
    y2 = jnp.dot(gated_sc[...], wd_ref[...], preferred_element_type=F32)
    o_ref[...] = x1 + mod[5:6] * _rms(y2, g_postf_ref[...])


def _ffn_call(x2, om, od, mod, g_post, g_pre, g_postf, wo, wa, wl, wc, bc, wd, batch, seq):
    m = x2.shape[0]
    tm = TM_FFN
    per_seq = seq // tm
    row = lambda i: (i, 0)
    return pl.pallas_call(
        functools.partial(_ffn_kernel, tiles_per_seq=per_seq),
        grid=(m // tm,),
        in_specs=[
            pl.BlockSpec((tm, D_MODEL), row),
            pl.BlockSpec((tm, WIDTH), row),
            pl.BlockSpec((tm, WIDTH), row),
            pl.BlockSpec((None, SUBLANES, D_MODEL), lambda i: (i // per_seq, 0, 0)),
            _resident((1, D_MODEL)), _resident((1, D_MODEL)), _resident((1, D_MODEL)),
            _resident((D_MODEL, D_MODEL)),
            _resident((D_MODEL, D_FF)), _resident((D_MODEL, D_FF)),
            _resident((CONV_WIDTH, D_FF)), _resident((1, D_FF)),
            _resident((D_FF, D_MODEL)),
        ],
        out_specs=pl.BlockSpec((tm, D_MODEL), row),
        out_shape=jax.ShapeDtypeStruct(x2.shape, F32),
        scratch_shapes=[
            pltpu.VMEM((tm + SUBLANES, FF_CHUNK), F32),
            pltpu.VMEM((tm, D_FF), BF16),
            pltpu.VMEM((SUBLANES, D_FF), F32),
        ],
        compiler_params=pltpu.CompilerParams(
            dimension_semantics=("arbitrary",), vmem_limit_bytes=VMEM_LIMIT),
        name="ffn",
    )(x2, om, od, mod, g_post, g_pre, g_postf, wo, wa, wl, wc, bc, wd)


def kernel(x, c, w_ada, b_ada, g_pre_mix, w_in, g_moba_out, g_dsa_out, w_out, g_post_mix,
           g_pre_ffn, w_up_act, w_up_lin, w_conv, b_conv, w_down, g_post_ffn):
    batch, seq, d = x.shape
    depth = w_ada.shape[0]
    n_blk = seq // MOBA_BLOCK
    assert d == D_MODEL and w_in.shape[-1] == PROJ_WIDTH
    assert seq % TM_PROJ == 0 and seq % TM_FFN == 0 and seq % (2 * TK) == 0 and TQ == TK
    assert min(DSA_TOPK, seq // 4) == DSA_TOPK and seq < 2 ** 13

    mod = _ada_call(c, w_ada, b_ada).reshape(depth, batch, N_MOD, D_MODEL)
    mod = jnp.pad(mod, ((0, 0), (0, 0), (0, SUBLANES - N_MOD), (0, 0)))
    tab = _rope_tables(seq)
    vec = lambda g: g.reshape(1, -1)

    x2 = x.reshape(batch * seq, D_MODEL)
    for l in range(depth):
        w_in_b = jnp.pad(w_in[l].astype(BF16), ((0, 0), (0, PROJ_PAD - PROJ_WIDTH)))
        wvt = jnp.stack([w_in[l][:, 2 * WIDTH:3 * WIDTH].T, w_in[l][:, 5 * WIDTH:6 * WIDTH].T]).astype(BF16)
        mq, mk, mvt, dq, dk, dvt, qi, kib, kw, km = _proj_call(
            x2, mod[l], vec(g_pre_mix[l]), w_in_b, wvt, tab, batch, seq)
        km = km.reshape(batch, n_blk, WIDTH)
        km = jnp.pad(km, ((0, 0), (0, -n_blk % BF16_ROWS), (0, 0)))
        o_moba = _moba_call(mq, mk, mvt, km, vec(g_moba_out[l]), batch, seq)
        o_dsa = _dsa_call(dq, dk, dvt, qi, kib, kw, vec(g_dsa_out[l]), batch, seq)
        x2 = _ffn_call(
            x2, o_moba, o_dsa, mod[l], vec(g_post_mix[l]), vec(g_pre_ffn[l]), vec(g_post_ffn[l]),
            w_out[l].astype(BF16), w_up_act[l].astype(BF16), w_up_lin[l].astype(BF16),
            w_conv[l], vec(b_conv[l]), w_down[l].astype(BF16), batch, seq)
    return x2.reshape(batch, seq, D_MODEL)
```

```python
import functools
import math

import jax
import jax.numpy as jnp
from jax import lax
from jax.experimental import pallas as pl
from jax.experimental.pallas import tpu as pltpu

F32 = jnp.float32
BF16 = jnp.bfloat16

D_MODEL = 1024
HEAD_DIM = 64
N_HEADS = 8
WIDTH = N_HEADS * HEAD_DIM
ROPE_THETA = 500000.0
ROPE_DIM = HEAD_DIM // 4
MOBA_BLOCK = 256
MOBA_TOPK = 3
N_IDX_HEADS = 8
IDX_DIM = 64
DSA_TOPK = 256
D_FF = 2816
CONV_WIDTH = 3
NORM_EPS = 1e-6
N_MOD = 6
PROJ_WIDTH = 6 * WIDTH + N_IDX_HEADS * IDX_DIM + IDX_DIM + N_IDX_HEADS

LANES = 128
SUBLANES = 8
BF16_ROWS = 16
PROJ_PAD = 29 * LANES
TAIL_COL = 7 * WIDTH
IDX_SCALE = IDX_DIM ** -0.5
Q_SCALE = HEAD_DIM ** -0.5 * math.log2(math.e)
W_SCALE = N_IDX_HEADS ** -0.5
NEG_INF = float("-inf")
POS_INF = float("inf")
M_INIT = -1e30
INT_MIN = -(2 ** 31)
HALF_MIN = -(2 ** 15)

TM_PROJ = 512
TM_FFN = 512
TQ = MOBA_BLOCK
TK = MOBA_BLOCK
FF_CHUNK = 256
HEAD_LOOKAHEAD = 4
VMEM_LIMIT = 52 * 1024 * 1024

_CONTRACT_LANES = (((1,), (1,)), ((), ()))


def _rms(x, g):
    return x * lax.rsqrt(jnp.mean(x * x, axis=-1, keepdims=True) + NORM_EPS) * g


def _dot_t(a, b):
    return lax.dot_general(a, b, _CONTRACT_LANES, preferred_element_type=F32)


def _dot(a, b):
    return jnp.dot(a, b, preferred_element_type=F32)


def _resident(shape):
    return pl.BlockSpec(shape, lambda *_: (0,) * len(shape), pipeline_mode=pl.Buffered(1))


def _ada_kernel(c_ref, w_ref, b_ref, o_ref):
    c = c_ref[...]
    c_act = (c * jax.nn.sigmoid(c)).astype(BF16)
    o_ref[...] = jnp.dot(c_act, w_ref[...].astype(BF16), preferred_element_type=F32) + b_ref[...]


def _ada_call(c, w_ada, b_ada):
    depth, d, n = w_ada.shape
    b = c.shape[0]
    tn = n // 4
    return pl.pallas_call(
        _ada_kernel,
        grid=(depth, n // tn),
        in_specs=[
            pl.BlockSpec((b, d), lambda l, j: (0, 0)),
            pl.BlockSpec((None, d, tn), lambda l, j: (l, 0, j)),
            pl.BlockSpec((None, 1, tn), lambda l, j: (l, 0, j)),
        ],
        out_specs=pl.BlockSpec((None, b, tn), lambda l, j: (l, 0, j)),
        out_shape=jax.ShapeDtypeStruct((depth, b, n), F32),
        compiler_params=pltpu.CompilerParams(vmem_limit_bytes=VMEM_LIMIT),
        name="ada",
    )(c, w_ada, b_ada.reshape(depth, 1, n))


def _rope_tables(seq):
    pos = jnp.arange(seq, dtype=F32)
    inv_freq = ROPE_THETA ** (-jnp.arange(0, ROPE_DIM, 2, dtype=F32) / ROPE_DIM)
    ang = pos[:, None] * inv_freq[None, :]
    cos, sin = jnp.cos(ang), jnp.sin(ang)
    half = ROPE_DIM // 2
    d = jnp.arange(LANES) % HEAD_DIM
    f = d % half
    t_self = jnp.where(d < ROPE_DIM, cos[:, f], 1.0)
    t_up = jnp.where(d < half, -sin[:, f], 0.0)
    t_dn = jnp.where((d >= half) & (d < ROPE_DIM), sin[:, f], 0.0)
    return jnp.concatenate([t_self, t_up, t_dn], axis=1).astype(F32)


def _proj_kernel(x_ref, mod_ref, g_ref, w_ref, wvt_ref, tab_ref,
                 mq_ref, mk_ref, mvt_ref, dq_ref, dk_ref, dvt_ref, qi_ref, kib_ref, kw_ref, km_ref):
    mod = mod_ref[...]
    h = _rms(x_ref[...], g_ref[...]) * (1.0 + mod[1:2]) + mod[0:1]
    hb = h.astype(BF16)
    tab = tab_ref[...]
    t_self, t_up, t_dn = tab[:, :LANES], tab[:, LANES:2 * LANES], tab[:, 2 * LANES:]
    half = ROPE_DIM // 2

    def rope(a):
        up = pltpu.roll(a, LANES - half, axis=1)
        dn = pltpu.roll(a, half, axis=1)
        return a * t_self + up * t_up + dn * t_dn

    def roped_group(g_idx, scale):
        a = jnp.dot(hb, w_ref[:, g_idx * WIDTH:(g_idx + 1) * WIDTH], preferred_element_type=F32)
        if scale is not None:
            a = a * scale
        return jnp.concatenate([rope(a[:, c * LANES:(c + 1) * LANES]) for c in range(WIDTH // LANES)], axis=1)

    def values_t(v_idx, out_ref):
        vt = _dot_t(wvt_ref[v_idx], hb).astype(BF16)
        for t in range(vt.shape[1] // TK):
            out_ref[t] = vt[:, t * TK:(t + 1) * TK]

    mq_ref[...] = roped_group(0, Q_SCALE).astype(BF16)
    mk = roped_group(1, None)
    mk_ref[...] = mk.astype(BF16)
    km_ref[...] = jnp.mean(mk.reshape(mk.shape[0] // MOBA_BLOCK, MOBA_BLOCK, WIDTH), axis=1)
    values_t(0, mvt_ref)
    dq_ref[...] = roped_group(3, Q_SCALE).astype(BF16)
    dk_ref[...] = roped_group(4, None).astype(BF16)
    values_t(1, dvt_ref)
    qi_ref[...] = roped_group(6, IDX_SCALE).astype(BF16)

    tail = jnp.dot(hb, w_ref[:, TAIL_COL:TAIL_COL + LANES], preferred_element_type=F32)
    lane = lax.broadcasted_iota(jnp.int32, tail.shape, 1)
    is_key = lane < IDX_DIM
    roped = rope(tail)
    kw_ref[...] = jnp.where(is_key, roped, tail)
    kib_ref[...] = jnp.where(is_key, roped, pltpu.roll(roped, IDX_DIM, axis=1)).astype(BF16)


def _proj_call(x2, mod, g_pre, w_in_b, wvt, tab, batch, seq):
    m = x2.shape[0]
    tm = TM_PROJ
    per_seq = seq // tm
    nblk = tm // MOBA_BLOCK
    row = lambda i: (i, 0)
    lead = lambda i: (i, 0, 0)
    wide = jax.ShapeDtypeStruct((m, WIDTH), BF16)
    wide_t = jax.ShapeDtypeStruct((m // TK, WIDTH, TK), BF16)
    wide_spec = pl.BlockSpec((tm, WIDTH), row)
    wide_t_spec = pl.BlockSpec((tm // TK, WIDTH, TK), lead)
    return pl.pallas_call(
        _proj_kernel,
        grid=(m // tm,),
        in_specs=[
            pl.BlockSpec((tm, D_MODEL), row),
            pl.BlockSpec((None, SUBLANES, D_MODEL), lambda i: (i // per_seq, 0, 0)),
            _resident((1, D_MODEL)),
            _resident((D_MODEL, PROJ_PAD)),
            _resident((2, WIDTH, D_MODEL)),
            pl.BlockSpec((tm, 3 * LANES), lambda i: (i % per_seq, 0)),
        ],
        out_specs=[wide_spec, wide_spec, wide_t_spec, wide_spec, wide_spec, wide_t_spec, wide_spec,
                   pl.BlockSpec((tm, LANES), row),
                   pl.BlockSpec((tm, LANES), row),
                   pl.BlockSpec((None, nblk, WIDTH), lead)],
        out_shape=[wide, wide, wide_t, wide, wide, wide_t, wide,
                   jax.ShapeDtypeStruct((m, LANES), BF16),
                   jax.ShapeDtypeStruct((m, LANES), F32),
                   jax.ShapeDtypeStruct((m // tm, nblk, WIDTH), F32)],
        compiler_params=pltpu.CompilerParams(
            dimension_semantics=("parallel",), vmem_limit_bytes=VMEM_LIMIT),
        name="proj",
    )(x2, mod, g_pre, w_in_b, wvt, tab)


def _mask_heads(q_ref, dst_sc):
    lane = lax.broadcasted_iota(jnp.int32, (TQ, LANES), 1)
    for h in range(N_HEADS):
        c, hh = divmod(h, 2)
        pair = q_ref[:, c * LANES:(c + 1) * LANES]
        keep = (lane < HEAD_DIM) if hh == 0 else (lane >= HEAD_DIM)
        dst_sc[h] = jnp.where(keep, pair, jnp.zeros_like(pair)).astype(F32).T.astype(BF16)


def _attend_tile(score_fn, cap_fn, kv_idx, n_sub, vt_ref, m_sc, acc_sc):
    ones_rows = jnp.ones((BF16_ROWS, n_sub * TK), BF16)
    scores = {h: score_fn(h) for h in range(min(HEAD_LOOKAHEAD, N_HEADS))}
    for h in range(N_HEADS):
        s = scores.pop(h)
        subs = [s[t * TK:(t + 1) * TK] for t in range(n_sub)]
        caps = [cap_fn(h, t, jnp.max(subs[t], axis=0, keepdims=True).astype(F32)) for t in range(n_sub)]
        m_old = m_sc[h]
        m_new = functools.reduce(jnp.maximum, caps, m_old)
        p = jnp.concatenate(
            [jnp.exp2(subs[t] - jnp.where(caps[t] > NEG_INF, m_new, POS_INF).astype(BF16)) for t in range(n_sub)],
            axis=0)
        alpha = jnp.exp2(m_old - m_new)
        m_sc[h] = m_new
        if h + HEAD_LOOKAHEAD < N_HEADS:
            scores[h + HEAD_LOOKAHEAD] = score_fn(h + HEAD_LOOKAHEAD)
        values_t = jnp.concatenate(
            [vt_ref[kv_idx + t, h * HEAD_DIM:(h + 1) * HEAD_DIM, :] for t in range(n_sub)], axis=1)
        vt_tile = jnp.concatenate([values_t, ones_rows], axis=0)
        acc_sc[h] = alpha * acc_sc[h] + jnp.dot(vt_tile, p, preferred_element_type=F32)


def _over_key_tiles(n_tiles, step):
    def pair(jj, _):
        step(2 * jj, 2)
        return 0

    lax.fori_loop(0, n_tiles // 2, pair, 0)

    @pl.when(n_tiles % 2 == 1)
    def _():
        step(n_tiles - 1, 1)


def _init_stats(m_sc, acc_sc):
    for h in range(N_HEADS):
        m_sc[h] = jnp.full((1, TQ), M_INIT, F32)
        acc_sc[h] = jnp.zeros((HEAD_DIM + BF16_ROWS, TQ), F32)


def _finish(acc_sc, g_ref, o_ref):
    o_t = jnp.concatenate(
        [acc_sc[h, :HEAD_DIM, :] * (1.0 / acc_sc[h, HEAD_DIM:HEAD_DIM + 1, :]) for h in range(N_HEADS)], axis=0)
    o_t = o_t * lax.rsqrt(jnp.mean(o_t * o_t, axis=0, keepdims=True) + NORM_EPS)
    o_ref[...] = (o_t.T * g_ref[...]).astype(BF16)


def _attn_scratch():
    return [
        pltpu.VMEM((N_HEADS, LANES, TQ), BF16),
        pltpu.VMEM((N_HEADS, 1, TQ), F32),
        pltpu.VMEM((N_HEADS, HEAD_DIM + BF16_ROWS, TQ), F32),
    ]


def _key_query_iotas():
    return (lax.broadcasted_iota(jnp.int32, (TK, TQ), 0), lax.broadcasted_iota(jnp.int32, (TK, TQ), 1))


def _moba_kernel(q_ref, k_ref, vt_ref, km_ref, g_ref, o_ref, qm_sc, m_sc, acc_sc, sel_sc):
    i = pl.program_id(1)
    n_blk = km_ref.shape[0]
    _mask_heads(q_ref, qm_sc)
    _init_stats(m_sc, acc_sc)

    blk = lax.broadcasted_iota(jnp.int32, (n_blk, TQ), 0)
    blk_f = blk.astype(F32)
    for h in range(N_HEADS):
        cols = slice((h // 2) * LANES, (h // 2 + 1) * LANES)
        gate = jnp.where(blk < i, _dot(km_ref[:, cols].astype(BF16), qm_sc[h]), NEG_INF)
        sel = jnp.zeros((n_blk, TQ), F32)
        for _ in range(MOBA_TOPK):
            top = jnp.max(gate, axis=0, keepdims=True)
            first = jnp.min(jnp.where(gate == top, blk_f, float(n_blk)), axis=0, keepdims=True)
            is_first = blk_f == first
            sel = jnp.where(is_first & (top > NEG_INF), 1.0, sel)
            gate = jnp.where(is_first, NEG_INF, gate)
        sel_sc[h] = sel

    def raw_scores(j, n_sub):
        rows = pl.ds(pl.multiple_of(j * TK, TK), n_sub * TK)
        return lambda h: _dot(k_ref[rows, (h // 2) * LANES:(h // 2 + 1) * LANES], qm_sc[h]).astype(BF16)

    def past_cap(j):
        return lambda h, t, top: jnp.where(sel_sc[h, pl.ds(j + t, 1), :] > 0.0, top, NEG_INF)

    def past_pair(jj, _):
        _attend_tile(raw_scores(2 * jj, 2), past_cap(2 * jj), 2 * jj, 2, vt_ref, m_sc, acc_sc)
        return 0

    lax.fori_loop(0, i // 2, past_pair, 0)

    key_pos, query_pos = _key_query_iotas()
    causal = key_pos <= query_pos

    @pl.when(i % 2 == 1)
    def _():
        scores = raw_scores(i - 1, 2)
        chosen_cap = past_cap(i - 1)

        def last_two(h):
            s = scores(h)
            return jnp.concatenate([s[:TK], jnp.where(causal, s[TK:], NEG_INF)], axis=0)

        _attend_tile(last_two, lambda h, t, top: top if t == 1 else chosen_cap(h, t, top),
                     i - 1, 2, vt_ref, m_sc, acc_sc)

    @pl.when(i % 2 == 0)
    def _():
        own_scores = raw_scores(i, 1)
        _attend_tile(lambda h: jnp.where(causal, own_scores(h), NEG_INF),
                     lambda h, t, top: top, i, 1, vt_ref, m_sc, acc_sc)

    _finish(acc_sc, g_ref, o_ref)


def _moba_call(mq, mk, mvt, km, g, batch, seq):
    per_seq = seq // TQ
    n_blk = km.shape[1]
    q_spec = pl.BlockSpec((TQ, WIDTH), lambda b, i: (b * per_seq + i, 0))
    return pl.pallas_call(
        _moba_kernel,
        grid=(batch, per_seq),
        in_specs=[
            q_spec,
            pl.BlockSpec((seq, WIDTH), lambda b, i: (b, 0)),
            pl.BlockSpec((seq // TK, WIDTH, TK), lambda b, i: (b, 0, 0)),
            pl.BlockSpec((None, n_blk, WIDTH), lambda b, i: (b, 0, 0)),
            _resident((1, WIDTH)),
        ],
        out_specs=q_spec,
        out_shape=jax.ShapeDtypeStruct(mq.shape, BF16),
        scratch_shapes=_attn_scratch() + [
            pltpu.VMEM((N_HEADS, n_blk, TQ), F32),
        ],
        compiler_params=pltpu.CompilerParams(
            dimension_semantics=("parallel", "parallel"), vmem_limit_bytes=VMEM_LIMIT),
        name="moba",
    )(mq, mk, mvt, km, g)


def _dsa_kernel(q_ref, k_ref, vt_ref, qi_ref, kib_ref, kw_ref, g_ref, o_ref,
                qm_sc, m_sc, acc_sc, qim_sc, w_sc, key_sc, half_sc):
    i = pl.program_id(1)
    n_kv = i + 1
    key_pos, query_pos = _key_query_iotas()
    _mask_heads(q_ref, qm_sc)
    _mask_heads(qi_ref, qim_sc)
    _init_stats(m_sc, acc_sc)
    w_sc[...] = kw_ref[...].T[IDX_DIM:IDX_DIM + N_IDX_HEADS, :] * W_SCALE

    def score_tile(j):
        rows = pl.ds(pl.multiple_of(j * TK, TK), TK)
        k_idx = kib_ref[rows, :]
        score = jnp.zeros((TK, TQ), F32)
        for h in range(N_IDX_HEADS):
            score = score + w_sc[h:h + 1, :] * jnp.maximum(_dot(k_idx, qim_sc[h]), 0.0)
        bits = pltpu.bitcast(score + 0.0, jnp.int32)
        key = jnp.where(bits < 0, bits ^ jnp.int32(0x7FFFFFFF), bits)
        admissible = (key_pos <= query_pos) | (j < i)
        key = jnp.where(admissible, key, jnp.int32(INT_MIN))
        key_sc[j] = key
        half_sc[j] = lax.shift_right_arithmetic(key, 16).astype(jnp.int16)

    def score_step(j, n_sub):
        for t in range(n_sub):
            score_tile(j + t)

    _over_key_tiles(n_kv, score_step)

    @pl.when(n_kv % 2 == 1)
    def _():
        half_sc[n_kv] = jnp.full((TK, TQ), HALF_MIN, jnp.int16)

    def count(pred):
        def step(j, acc):
            return acc + jnp.sum(jnp.where(pred(key_sc[j], j), 1.0, 0.0), axis=0, keepdims=True)
        return lax.fori_loop(0, n_kv, step, jnp.zeros((1, TQ), F32))

    def count_half(pred):
        def step(jj, acc):
            slabs = []
            for t in range(2):
                ones = jnp.where(pred(half_sc[2 * jj + t]), jnp.int16(1), jnp.int16(0))
                slabs += [ones[r * BF16_ROWS:(r + 1) * BF16_ROWS] for r in range(TK // BF16_ROWS)]
            while len(slabs) > 1:
                slabs = [a + b for a, b in zip(slabs[::2], slabs[1::2])]
            return acc + slabs[0]
        acc = lax.fori_loop(0, (n_kv + 1) // 2, step, jnp.zeros((BF16_ROWS, TQ), jnp.int16))
        return jnp.sum(acc.astype(jnp.int32), axis=0, keepdims=True)

    def bisect_half(need):
        def bit_step(b, carry):
            thr, n_at_thr = carry
            cand = thr + lax.shift_left(jnp.int32(1), 15 - b)
            cand16 = cand.astype(jnp.int16)
            cnt = count_half(lambda half: half >= cand16)
            keep = cnt >= need
            return jnp.where(keep, cand, thr), jnp.where(keep, cnt, n_at_thr)
        return lax.fori_loop(0, 16, bit_step,
                             (jnp.full((1, TQ), HALF_MIN, jnp.int32), jnp.zeros((1, TQ), jnp.int32)))

    hi, n_ge_hi = bisect_half(jnp.full((1, TQ), DSA_TOPK, jnp.int32))
    hi16 = hi.astype(jnp.int16)
    n_gt_hi = count_half(lambda half: half > hi16)
    need_lo = DSA_TOPK - n_gt_hi

    def to_lower_half(j, _):
        lower = ((key_sc[j] & jnp.int32(0xFFFF)) + jnp.int32(HALF_MIN)).astype(jnp.int16)
        half_sc[j] = jnp.where(half_sc[j] == hi16, lower, jnp.int16(HALF_MIN))
        return 0

    lax.fori_loop(0, n_kv, to_lower_half, 0)
    lo, n_ge_lo = bisect_half(need_lo)
    thr = hi * jnp.int32(2 ** 16) + (lo - jnp.int32(HALF_MIN))

    n_ge = n_gt_hi + jnp.where(lo > jnp.int32(HALF_MIN), n_ge_lo, n_ge_hi - n_gt_hi)
    excess = (n_ge > DSA_TOPK) & (thr > jnp.int32(INT_MIN))

    @pl.when(jnp.max(jnp.where(excess, 1.0, 0.0)) > 0.0)
    def _():
        need = float(DSA_TOPK) - count(lambda k, j: k > thr)

        def idx_step(b, cut):
            cand = cut + lax.shift_left(jnp.int32(1), 12 - b)
            cnt = count(lambda k, j: (k == thr) & (j * TK + key_pos < cand))
            return jnp.where(cnt <= need, cand, cut)

        cut = lax.fori_loop(0, 13, idx_step, jnp.zeros((1, TQ), jnp.int32))

        def demote(j, _):
            k = key_sc[j]
            drop = excess & (k == thr) & (j * TK + key_pos >= cut)
            key_sc[j] = jnp.where(drop, thr - 1, k)
            return 0

        lax.fori_loop(0, n_kv, demote, 0)

    thr = jnp.maximum(thr, jnp.int32(INT_MIN + 1))

    def attn_step(j, n_sub):
        rows = pl.ds(pl.multiple_of(j * TK, TK), n_sub * TK)
        bias = jnp.concatenate(
            [jnp.where(key_sc[j + t] >= thr, 0.0, NEG_INF) for t in range(n_sub)], axis=0).astype(BF16)
        _attend_tile(
            lambda h: _dot(k_ref[rows, (h // 2) * LANES:(h // 2 + 1) * LANES], qm_sc[h]).astype(BF16) + bias,
            lambda h, t, top: top, j, n_sub, vt_ref, m_sc, acc_sc)

    _over_key_tiles(n_kv, attn_step)
    _finish(acc_sc, g_ref, o_ref)


def _dsa_call(dq, dk, dvt, qi, kib, kw, g, batch, seq):
    per_seq = seq // TQ
    q_spec = pl.BlockSpec((TQ, WIDTH), lambda b, i: (b * per_seq + i, 0))
    return pl.pallas_call(
        _dsa_kernel,
        grid=(batch, per_seq),
        in_specs=[
            q_spec,
            pl.BlockSpec((seq, WIDTH), lambda b, i: (b, 0)),
            pl.BlockSpec((seq // TK, WIDTH, TK), lambda b, i: (b, 0, 0)),
            q_spec,
            pl.BlockSpec((seq, LANES), lambda b, i: (b, 0)),
            pl.BlockSpec((TQ, LANES), lambda b, i: (b * per_seq + i, 0)),
            _resident((1, WIDTH)),
        ],
        out_specs=q_spec,
        out_shape=jax.ShapeDtypeStruct(dq.shape, BF16),
        scratch_shapes=_attn_scratch() + [
            pltpu.VMEM((N_IDX_HEADS, LANES, TQ), BF16),
            pltpu.VMEM((N_IDX_HEADS, TQ), F32),
            pltpu.VMEM((per_seq, TK, TQ), jnp.int32),
            pltpu.VMEM((per_seq, TK, TQ), jnp.int16),
        ],
        compiler_params=pltpu.CompilerParams(
            dimension_semantics=("parallel", "parallel"), vmem_limit_bytes=VMEM_LIMIT),
        name="dsa",
    )(dq, dk, dvt, qi, kib, kw, g)


def _ffn_kernel(x_ref, om_ref, od_ref, mod_ref, g_post_ref, g_pre_ref, g_postf_ref,
                wo_ref, wa_ref, wl_ref, wc_ref, bc_ref, wd_ref, o_ref,
                apad_sc, gated_sc, tail_sc, *, tiles_per_seq):
    tm = x_ref.shape[0]
    first = (pl.program_id(0) % tiles_per_seq) == 0
    mod = mod_ref[...]
    o = jnp.concatenate([om_ref[...], od_ref[...]], axis=1)
    y = jnp.dot(o, wo_ref[...], preferred_element_type=F32)
    x1 = x_ref[...] + mod[2:3] * _rms(y, g_post_ref[...])
    hb = (_rms(x1, g_pre_ref[...]) * (1.0 + mod[4:5]) + mod[3:4]).astype(BF16)
    for c in range(D_FF // FF_CHUNK):
        cols = slice(c * FF_CHUNK, (c + 1) * FF_CHUNK)
        a = jnp.dot(hb, wa_ref[:, cols], preferred_element_type=F32)
        apad_sc[0:SUBLANES, :] = jnp.where(first, 0.0, tail_sc[:, cols])
        apad_sc[SUBLANES:, :] = a
        tail_sc[:, cols] = a[tm - SUBLANES:, :]
        wc = wc_ref[:, cols]
        conv = (wc[0:1] * apad_sc[SUBLANES - 2:SUBLANES - 2 + tm, :]
                + wc[1:2] * apad_sc[SUBLANES - 1:SUBLANES - 1 + tm, :]
                + wc[2:3] * a + bc_ref[:, cols])
        u = jnp.dot(hb, wl_ref[:, cols], preferred_element_type=F32)
        gated_sc[:, cols] = (jax.nn.gelu(conv) * u).astype(BF16)
    y2 = jnp.dot(gated_sc[...], wd_ref[...], preferred_element_type=F32)
    o_ref[...] = x1 + mod[5:6] * _rms(y2, g_postf_ref[...])


def _ffn_call(x2, om, od, mod, g_post, g_pre, g_postf, wo, wa, wl, wc, bc, wd, batch, seq):
    m = x2.shape[0]
    tm = TM_FFN
    per_seq = seq // tm
    row = lambda i: (i, 0)
    return pl.pallas_call(
        functools.partial(_ffn_kernel, tiles_per_seq=per_seq),
        grid=(m // tm,),
        in_specs=[
            pl.BlockSpec((tm, D_MODEL), row),
            pl.BlockSpec((tm, WIDTH), row),
            pl.BlockSpec((tm, WIDTH), row),
            pl.BlockSpec((None, SUBLANES, D_MODEL), lambda i: (i // per_seq, 0, 0)),
            _resident((1, D_MODEL)), _resident((1, D_MODEL)), _resident((1, D_MODEL)),
            _resident((D_MODEL, D_MODEL)),
            _resident((D_MODEL, D_FF)), _resident((D_MODEL, D_FF)),
            _resident((CONV_WIDTH, D_FF)), _resident((1, D_FF)),
            _resident((D_FF, D_MODEL)),
        ],
        out_specs=pl.BlockSpec((tm, D_MODEL), row),
        out_shape=jax.ShapeDtypeStruct(x2.shape, F32),
        scratch_shapes=[
            pltpu.VMEM((tm + SUBLANES, FF_CHUNK), F32),
            pltpu.VMEM((tm, D_FF), BF16),
            pltpu.VMEM((SUBLANES, D_FF), F32),
        ],
        compiler_params=pltpu.CompilerParams(
            dimension_semantics=("arbitrary",), vmem_limit_bytes=VMEM_LIMIT),
        name="ffn",
    )(x2, om, od, mod, g_post, g_pre, g_postf, wo, wa, wl, wc, bc, wd)


def kernel(x, c, w_ada, b_ada, g_pre_mix, w_in, g_moba_out, g_dsa_out, w_out, g_post_mix,
           g_pre_ffn, w_up_act, w_up_lin, w_conv, b_conv, w_down, g_post_ffn):
    batch, seq, d = x.shape
    depth = w_ada.shape[0]
    n_blk = seq // MOBA_BLOCK
    assert d == D_MODEL and w_in.shape[-1] == PROJ_WIDTH
    assert seq % TM_PROJ == 0 and seq % TM_FFN == 0 and seq % (2 * TK) == 0 and TQ == TK
    assert min(DSA_TOPK, seq // 4) == DSA_TOPK and seq < 2 ** 13

    mod = _ada_call(c, w_ada, b_ada).reshape(depth, batch, N_MOD, D_MODEL)
    mod = jnp.pad(mod, ((0, 0), (0, 0), (0, SUBLANES - N_MOD), (0, 0)))
    tab = _rope_tables(seq)
    vec = lambda g: g.reshape(1, -1)

    x2 = x.reshape(batch * seq, D_MODEL)
    for l in range(depth):
        w_in_b = jnp.pad(w_in[l].astype(BF16), ((0, 0), (0, PROJ_PAD - PROJ_WIDTH)))
        wvt = jnp.stack([w_in[l][:, 2 * WIDTH:3 * WIDTH].T, w_in[l][:, 5 * WIDTH:6 * WIDTH].T]).astype(BF16)
        mq, mk, mvt, dq, dk, dvt, qi, kib, kw, km = _proj_call(
            x2, mod[l], vec(g_pre_mix[l]), w_in_b, wvt, tab, batch, seq)
        km = km.reshape(batch, n_blk, WIDTH)
        km = jnp.pad(km, ((0, 0), (0, -n_blk % BF16_ROWS), (0, 0)))
        o_moba = _moba_call(mq, mk, mvt, km, vec(g_moba_out[l]), batch, seq)
        o_dsa = _dsa_call(dq, dk, dvt, qi, kib, kw, vec(g_dsa_out[l]), batch, seq)
        x2 = _ffn_call(
            x2, o_moba, o_dsa, mod[l], vec(g_post_mix[l]), vec(g_pre_ffn[l]), vec(g_post_ffn[l]),
            w_out[l].astype(BF16), w_up_act[l].astype(BF16), w_up_lin[l].astype(BF16),
            w_conv[l], vec(b_conv[l]), w_down[l].astype(BF16), batch, seq)
    return x2.reshape(batch, seq, D_MODEL)
```

```python
import functools
import math

import jax
import jax.numpy as jnp
from jax import lax
from jax.experimental import pallas as pl
from jax.experimental.pallas import tpu as pltpu

F32 = jnp.float32
BF16 = jnp.bfloat16

D_MODEL = 1024
HEAD_DIM = 64
N_HEADS = 8
WIDTH = N_HEADS * HEAD_DIM
ROPE_THETA = 500000.0
ROPE_DIM = HEAD_DIM // 4
MOBA_BLOCK = 256
MOBA_TOPK = 3
N_IDX_HEADS = 8
IDX_DIM = 64
DSA_TOPK = 256
D_FF = 2816
CONV_WIDTH = 3
NORM_EPS = 1e-6
N_MOD = 6
PROJ_WIDTH = 6 * WIDTH + N_IDX_HEADS * IDX_DIM + IDX_DIM + N_IDX_HEADS

LANES = 128
SUBLANES = 8
BF16_ROWS = 16
PROJ_PAD = 29 * LANES
TAIL_COL = 7 * WIDTH
IDX_SCALE = IDX_DIM ** -0.5
Q_SCALE = HEAD_DIM ** -0.5 * math.log2(math.e)
W_SCALE = N_IDX_HEADS ** -0.5
NEG_INF = float("-inf")
POS_INF = float("inf")
M_INIT = -1e30
INT_MIN = -(2 ** 31)
HALF_MIN = -(2 ** 15)

TM_PROJ = 512
TM_FFN = 512
TQ = MOBA_BLOCK
TK = MOBA_BLOCK
FF_CHUNK = 256
HEAD_LOOKAHEAD = 4
VMEM_LIMIT = 52 * 1024 * 1024

_CONTRACT_LANES = (((1,), (1,)), ((), ()))


def _rms(x, g):
    return x * lax.rsqrt(jnp.mean(x * x, axis=-1, keepdims=True) + NORM_EPS) * g


def _dot_t(a, b):
    return lax.dot_general(a, b, _CONTRACT_LANES, preferred_element_type=F32)


def _dot(a, b):
    return jnp.dot(a, b, preferred_element_type=F32)


def _resident(shape):
    return pl.BlockSpec(shape, lambda *_: (0,) * len(shape), pipeline_mode=pl.Buffered(1))


def _ada_kernel(c_ref, w_ref, b_ref, o_ref):
    c = c_ref[...]
    c_act = (c * jax.nn.sigmoid(c)).astype(BF16)
    o_ref[...] = jnp.dot(c_act, w_ref[...].astype(BF16), preferred_element_type=F32) + b_ref[...]


def _ada_call(c, w_ada, b_ada):
    depth, d, n = w_ada.shape
    b = c.shape[0]
    tn = n // 4
    return pl.pallas_call(
        _ada_kernel,
        grid=(depth, n // tn),
        in_specs=[
            pl.BlockSpec((b, d), lambda l, j: (0, 0)),
            pl.BlockSpec((None, d, tn), lambda l, j: (l, 0, j)),
            pl.BlockSpec((None, 1, tn), lambda l, j: (l, 0, j)),
        ],
        out_specs=pl.BlockSpec((None, b, tn), lambda l, j: (l, 0, j)),
        out_shape=jax.ShapeDtypeStruct((depth, b, n), F32),
        compiler_params=pltpu.CompilerParams(vmem_limit_bytes=VMEM_LIMIT),
        name="ada",
    )(c, w_ada, b_ada.reshape(depth, 1, n))


def _rope_tables(seq):
    pos = jnp.arange(seq, dtype=F32)
    inv_freq = ROPE_THETA ** (-jnp.arange(0, ROPE_DIM, 2, dtype=F32) / ROPE_DIM)
    ang = pos[:, None] * inv_freq[None, :]
    cos, sin = jnp.cos(ang), jnp.sin(ang)
    half = ROPE_DIM // 2
    d = jnp.arange(LANES) % HEAD_DIM
    f = d % half
    t_self = jnp.where(d < ROPE_DIM, cos[:, f], 1.0)
    t_up = jnp.where(d < half, -sin[:, f], 0.0)
    t_dn = jnp.where((d >= half) & (d < ROPE_DIM), sin[:, f], 0.0)
    return jnp.concatenate([t_self, t_up, t_dn], axis=1).astype(F32)


def _proj_kernel(x_ref, mod_ref, g_ref, w_ref, wvt_ref, tab_ref,
                 mq_ref, mk_ref, mvt_ref, dq_ref, dk_ref, dvt_ref, qi_ref, kib_ref, kw_ref, km_ref):
    mod = mod_ref[...]
    h = _rms(x_ref[...], g_ref[...]) * (1.0 + mod[1:2]) + mod[0:1]
    hb = h.astype(BF16)
    tab = tab_ref[...]
    t_self, t_up, t_dn = tab[:, :LANES], tab[:, LANES:2 * LANES], tab[:, 2 * LANES:]
    half = ROPE_DIM // 2

    def rope(a):
        up = pltpu.roll(a, LANES - half, axis=1)
        dn = pltpu.roll(a, half, axis=1)
        return a * t_self + up * t_up + dn * t_dn

    def roped_group(g_idx, scale):
        a = jnp.dot(hb, w_ref[:, g_idx * WIDTH:(g_idx + 1) * WIDTH], preferred_element_type=F32)
        if scale is not None:
            a = a * scale
        return jnp.concatenate([rope(a[:, c * LANES:(c + 1) * LANES]) for c in range(WIDTH // LANES)], axis=1)

    def values_t(v_idx, out_ref):
        vt = _dot_t(wvt_ref[v_idx], hb).astype(BF16)
        for t in range(vt.shape[1] // TK):
            out_ref[t] = vt[:, t * TK:(t + 1) * TK]

    mq_ref[...] = roped_group(0, Q_SCALE).astype(BF16)
    mk = roped_group(1, None)
    mk_ref[...] = mk.astype(BF16)
    km_ref[...] = jnp.mean(mk.reshape(mk.shape[0] // MOBA_BLOCK, MOBA_BLOCK, WIDTH), axis=1)
    values_t(0, mvt_ref)
    dq_ref[...] = roped_group(3, Q_SCALE).astype(BF16)
    dk_ref[...] = roped_group(4, None).astype(BF16)
    values_t(1, dvt_ref)
    qi_ref[...] = roped_group(6, IDX_SCALE).astype(BF16)

    tail = jnp.dot(hb, w_ref[:, TAIL_COL:TAIL_COL + LANES], preferred_element_type=F32)
    lane = lax.broadcasted_iota(jnp.int32, tail.shape, 1)
    is_key = lane < IDX_DIM
    roped = rope(tail)
    kw_ref[...] = jnp.where(is_key, roped, tail)
    kib_ref[...] = jnp.where(is_key, roped, pltpu.roll(roped, IDX_DIM, axis=1)).astype(BF16)


def _proj_call(x2, mod, g_pre, w_in_b, wvt, tab, batch, seq):
    m = x2.shape[0]
    tm = TM_PROJ
    per_seq = seq // tm
    nblk = tm // MOBA_BLOCK
    row = lambda i: (i, 0)
    lead = lambda i: (i, 0, 0)
    wide = jax.ShapeDtypeStruct((m, WIDTH), BF16)
    wide_t = jax.ShapeDtypeStruct((m // TK, WIDTH, TK), BF16)
    wide_spec = pl.BlockSpec((tm, WIDTH), row)
    wide_t_spec = pl.BlockSpec((tm // TK, WIDTH, TK), lead)
    return pl.pallas_call(
        _proj_kernel,
        grid=(m // tm,),
        in_specs=[
            pl.BlockSpec((tm, D_MODEL), row),
            pl.BlockSpec((None, SUBLANES, D_MODEL), lambda i: (i // per_seq, 0, 0)),
            _resident((1, D_MODEL)),
            _resident((D_MODEL, PROJ_PAD)),
            _resident((2, WIDTH, D_MODEL)),
            pl.BlockSpec((tm, 3 * LANES), lambda i: (i % per_seq, 0)),
        ],
        out_specs=[wide_spec, wide_spec, wide_t_spec, wide_spec, wide_spec, wide_t_spec, wide_spec,
                   pl.BlockSpec((tm, LANES), row),
                   pl.BlockSpec((tm, LANES), row),
                   pl.BlockSpec((None, nblk, WIDTH), lead)],
        out_shape=[wide, wide, wide_t, wide, wide, wide_t, wide,
                   jax.ShapeDtypeStruct((m, LANES), BF16),
                   jax.ShapeDtypeStruct((m, LANES), F32),
                   jax.ShapeDtypeStruct((m // tm, nblk, WIDTH), F32)],
        compiler_params=pltpu.CompilerParams(
            dimension_semantics=("parallel",), vmem_limit_bytes=VMEM_LIMIT),
        name="proj",
    )(x2, mod, g_pre, w_in_b, wvt, tab)


def _mask_heads(q_ref, dst_sc):
    dim = lax.broadcasted_iota(jnp.int32, (LANES, TQ), 0)
    for c in range(N_HEADS // 2):
        pair_t = q_ref[:, c * LANES:(c + 1) * LANES].astype(F32).T
        dst_sc[2 * c] = jnp.where(dim < HEAD_DIM, pair_t, 0.0).astype(BF16)
        dst_sc[2 * c + 1] = jnp.where(dim >= HEAD_DIM, pair_t, 0.0).astype(BF16)


def _attend_tile(score_fn, cap_fn, kv_idx, n_sub, vt_ref, m_sc, acc_sc):
    ones_rows = jnp.ones((BF16_ROWS, n_sub * TK), BF16)
    scores = {h: score_fn(h) for h in range(min(HEAD_LOOKAHEAD, N_HEADS))}
    for h in range(N_HEADS):
        s = scores.pop(h)
        subs = [s[t * TK:(t + 1) * TK] for t in range(n_sub)]
        caps = [cap_fn(h, t, jnp.max(subs[t], axis=0, keepdims=True).astype(F32)) for t in range(n_sub)]
        m_old = m_sc[h]
        m_new = functools.reduce(jnp.maximum, caps, m_old)
        p = jnp.concatenate(
            [jnp.exp2(subs[t] - jnp.where(caps[t] > NEG_INF, m_new, POS_INF).astype(BF16)) for t in range(n_sub)],
            axis=0)
        alpha = jnp.exp2(m_old - m_new)
        m_sc[h] = m_new
        if h + HEAD_LOOKAHEAD < N_HEADS:
            scores[h + HEAD_LOOKAHEAD] = score_fn(h + HEAD_LOOKAHEAD)
        values_t = jnp.concatenate(
            [vt_ref[kv_idx + t, h * HEAD_DIM:(h + 1) * HEAD_DIM, :] for t in range(n_sub)], axis=1)
        vt_tile = jnp.concatenate([values_t, ones_rows], axis=0)
        acc_sc[h] = alpha * acc_sc[h] + jnp.dot(vt_tile, p, preferred_element_type=F32)


def _over_key_tiles(n_tiles, step):
    def pair(jj, _):
        step(2 * jj, 2)
        return 0

    lax.fori_loop(0, n_tiles // 2, pair, 0)

    @pl.when(n_tiles % 2 == 1)
    def _():
        step(n_tiles - 1, 1)


def _init_stats(m_sc, acc_sc):
    for h in range(N_HEADS):
        m_sc[h] = jnp.full((1, TQ), M_INIT, F32)
        acc_sc[h] = jnp.zeros((HEAD_DIM + BF16_ROWS, TQ), F32)


def _finish(acc_sc, g_ref, o_ref):
    o_t = jnp.concatenate(
        [acc_sc[h, :HEAD_DIM, :] * (1.0 / acc_sc[h, HEAD_DIM:HEAD_DIM + 1, :]) for h in range(N_HEADS)], axis=0)
    o_t = o_t * lax.rsqrt(jnp.mean(o_t * o_t, axis=0, keepdims=True) + NORM_EPS)
    o_ref[...] = (o_t.T * g_ref[...]).astype(BF16)


def _attn_scratch():
    return [
        pltpu.VMEM((N_HEADS, LANES, TQ), BF16),
        pltpu.VMEM((N_HEADS, 1, TQ), F32),
        pltpu.VMEM((N_HEADS, HEAD_DIM + BF16_ROWS, TQ), F32),
    ]


def _key_query_iotas():
    return (lax.broadcasted_iota(jnp.int32, (TK, TQ), 0), lax.broadcasted_iota(jnp.int32, (TK, TQ), 1))


def _moba_kernel(q_ref, k_ref, vt_ref, km_ref, g_ref, o_ref, qm_sc, m_sc, acc_sc, sel_sc):
    i = pl.program_id(1)
    n_blk = km_ref.shape[0]
    _mask_heads(q_ref, qm_sc)
    _init_stats(m_sc, acc_sc)

    blk = lax.broadcasted_iota(jnp.int32, (n_blk, TQ), 0)
    blk_f = blk.astype(F32)
    for h in range(N_HEADS):
        cols = slice((h // 2) * LANES, (h // 2 + 1) * LANES)
        gate = jnp.where(blk < i, _dot(km_ref[:, cols].astype(BF16), qm_sc[h]), NEG_INF)
        sel = jnp.zeros((n_blk, TQ), F32)
        for _ in range(MOBA_TOPK):
            top = jnp.max(gate, axis=0, keepdims=True)
            first = jnp.min(jnp.where(gate == top, blk_f, float(n_blk)), axis=0, keepdims=True)
            is_first = blk_f == first
            sel = jnp.where(is_first & (top > NEG_INF), 1.0, sel)
            gate = jnp.where(is_first, NEG_INF, gate)
        sel_sc[h] = sel

    def raw_scores(j, n_sub):
        rows = pl.ds(pl.multiple_of(j * TK, TK), n_sub * TK)
        return lambda h: _dot(k_ref[rows, (h // 2) * LANES:(h // 2 + 1) * LANES], qm_sc[h]).astype(BF16)

    def past_cap(j):
        return lambda h, t, top: jnp.where(sel_sc[h, pl.ds(j + t, 1), :] > 0.0, top, NEG_INF)

    def past_pair(jj, _):
        _attend_tile(raw_scores(2 * jj, 2), past_cap(2 * jj), 2 * jj, 2, vt_ref, m_sc, acc_sc)
        return 0

    lax.fori_loop(0, i // 2, past_pair, 0)

    key_pos, query_pos = _key_query_iotas()
    causal = key_pos <= query_pos

    @pl.when(i % 2 == 1)
    def _():
        scores = raw_scores(i - 1, 2)
        chosen_cap = past_cap(i - 1)

        def last_two(h):
            s = scores(h)
            return jnp.concatenate([s[:TK], jnp.where(causal, s[TK:], NEG_INF)], axis=0)

        _attend_tile(last_two, lambda h, t, top: top if t == 1 else chosen_cap(h, t, top),
                     i - 1, 2, vt_ref, m_sc, acc_sc)

    @pl.when(i % 2 == 0)
    def _():
        own_scores = raw_scores(i, 1)
        _attend_tile(lambda h: jnp.where(causal, own_scores(h), NEG_INF),
                     lambda h, t, top: top, i, 1, vt_ref, m_sc, acc_sc)

    _finish(acc_sc, g_ref, o_ref)


def _moba_call(mq, mk, mvt, km, g, batch, seq):
    per_seq = seq // TQ
    n_blk = km.shape[1]
    q_spec = pl.BlockSpec((TQ, WIDTH), lambda b, i: (b * per_seq + i, 0))
    return pl.pallas_call(
        _moba_kernel,
        grid=(batch, per_seq),
        in_specs=[
            q_spec,
            pl.BlockSpec((seq, WIDTH), lambda b, i: (b, 0)),
            pl.BlockSpec((seq // TK, WIDTH, TK), lambda b, i: (b, 0, 0)),
            pl.BlockSpec((None, n_blk, WIDTH), lambda b, i: (b, 0, 0)),
            _resident((1, WIDTH)),
        ],
        out_specs=q_spec,
        out_shape=jax.ShapeDtypeStruct(mq.shape, BF16),
        scratch_shapes=_attn_scratch() + [
            pltpu.VMEM((N_HEADS, n_blk, TQ), F32),
        ],
        compiler_params=pltpu.CompilerParams(
            dimension_semantics=("parallel", "parallel"), vmem_limit_bytes=VMEM_LIMIT),
        name="moba",
    )(mq, mk, mvt, km, g)


def _dsa_kernel(q_ref, k_ref, vt_ref, qi_ref, kib_ref, kw_ref, g_ref, o_ref,
                qm_sc, m_sc, acc_sc, qim_sc, w_sc, key_sc, half_sc):
    i = pl.program_id(1)
    n_kv = i + 1
    key_pos, query_pos = _key_query_iotas()
    _mask_heads(q_ref, qm_sc)
    _mask_heads(qi_ref, qim_sc)
    _init_stats(m_sc, acc_sc)
    w_sc[...] = kw_ref[...].T[IDX_DIM:IDX_DIM + N_IDX_HEADS, :] * W_SCALE

    def score_tile(j):
        rows = pl.ds(pl.multiple_of(j * TK, TK), TK)
        k_idx = kib_ref[rows, :]
        score = jnp.zeros((TK, TQ), F32)
        for h in range(N_IDX_HEADS):
            score = score + w_sc[h:h + 1, :] * jnp.maximum(_dot(k_idx, qim_sc[h]), 0.0)
        bits = pltpu.bitcast(score + 0.0, jnp.int32)
        key = jnp.where(bits < 0, bits ^ jnp.int32(0x7FFFFFFF), bits)
        admissible = (key_pos <= query_pos) | (j < i)
        key = jnp.where(admissible, key, jnp.int32(INT_MIN))
        key_sc[j] = key
        half_sc[j] = lax.shift_right_arithmetic(key, 16).astype(jnp.int16)

    def score_step(j, n_sub):
        for t in range(n_sub):
            score_tile(j + t)

    _over_key_tiles(n_kv, score_step)

    @pl.when(n_kv % 2 == 1)
    def _():
        half_sc[n_kv] = jnp.full((TK, TQ), HALF_MIN, jnp.int16)

    def count(pred):
        def step(j, acc):
            return acc + jnp.sum(jnp.where(pred(key_sc[j], j), 1.0, 0.0), axis=0, keepdims=True)
        return lax.fori_loop(0, n_kv, step, jnp.zeros((1, TQ), F32))

    def count_half(pred):
        def step(jj, acc):
            slabs = []
            for t in range(2):
                ones = jnp.where(pred(half_sc[2 * jj + t]), jnp.int16(1), jnp.int16(0))
                slabs += [ones[r * BF16_ROWS:(r + 1) * BF16_ROWS] for r in range(TK // BF16_ROWS)]
            while len(slabs) > 1:
                slabs = [a + b for a, b in zip(slabs[::2], slabs[1::2])]
            return acc + slabs[0]
        acc = lax.fori_loop(0, (n_kv + 1) // 2, step, jnp.zeros((BF16_ROWS, TQ), jnp.int16))
        return jnp.sum(acc.astype(jnp.int32), axis=0, keepdims=True)

    def bisect_half(need):
        def bit_step(b, carry):
            thr, n_at_thr = carry
            cand = thr + lax.shift_left(jnp.int32(1), 15 - b)
            cand16 = cand.astype(jnp.int16)
            cnt = count_half(lambda half: half >= cand16)
            keep = cnt >= need
            return jnp.where(keep, cand, thr), jnp.where(keep, cnt, n_at_thr)
        return lax.fori_loop(0, 16, bit_step,
                             (jnp.full((1, TQ), HALF_MIN, jnp.int32), jnp.zeros((1, TQ), jnp.int32)))

    hi, n_ge_hi = bisect_half(jnp.full((1, TQ), DSA_TOPK, jnp.int32))
    hi16 = hi.astype(jnp.int16)
    n_gt_hi = count_half(lambda half: half > hi16)
    need_lo = DSA_TOPK - n_gt_hi

    def to_lower_half(j, _):
        lower = ((key_sc[j] & jnp.int32(0xFFFF)) + jnp.int32(HALF_MIN)).astype(jnp.int16)
        half_sc[j] = jnp.where(half_sc[j] == hi16, lower, jnp.int16(HALF_MIN))
        return 0

    lax.fori_loop(0, n_kv, to_lower_half, 0)
    lo, n_ge_lo = bisect_half(need_lo)
    thr = hi * jnp.int32(2 ** 16) + (lo - jnp.int32(HALF_MIN))

    n_ge = n_gt_hi + jnp.where(lo > jnp.int32(HALF_MIN), n_ge_lo, n_ge_hi - n_gt_hi)
    excess = (n_ge > DSA_TOPK) & (thr > jnp.int32(INT_MIN))

    @pl.when(jnp.max(jnp.where(excess, 1.0, 0.0)) > 0.0)
    def _():
        need = float(DSA_TOPK) - count(lambda k, j: k > thr)

        def idx_step(b, cut):
            cand = cut + lax.shift_left(jnp.int32(1), 12 - b)
            cnt = count(lambda k, j: (k == thr) & (j * TK + key_pos < cand))
            return jnp.where(cnt <= need, cand, cut)

        cut = lax.fori_loop(0, 13, idx_step, jnp.zeros((1, TQ), jnp.int32))

        def demote(j, _):
            k = key_sc[j]
            drop = excess & (k == thr) & (j * TK + key_pos >= cut)
            key_sc[j] = jnp.where(drop, thr - 1, k)
            return 0

        lax.fori_loop(0, n_kv, demote, 0)

    thr = jnp.maximum(thr, jnp.int32(INT_MIN + 1))

    def attn_step(j, n_sub):
        rows = pl.ds(pl.multiple_of(j * TK, TK), n_sub * TK)
        bias = jnp.concatenate(
            [jnp.where(key_sc[j + t] >= thr, 0.0, NEG_INF) for t in range(n_sub)], axis=0).astype(BF16)
        _attend_tile(
            lambda h: _dot(k_ref[rows, (h // 2) * LANES:(h // 2 + 1) * LANES], qm_sc[h]).astype(BF16) + bias,
            lambda h, t, top: top, j, n_sub, vt_ref, m_sc, acc_sc)

    _over_key_tiles(n_kv, attn_step)
    _finish(acc_sc, g_ref, o_ref)


def _dsa_call(dq, dk, dvt, qi, kib, kw, g, batch, seq):
    per_seq = seq // TQ
    q_spec = pl.BlockSpec((TQ, WIDTH), lambda b, i: (b * per_seq + i, 0))
    return pl.pallas_call(
        _dsa_kernel,
        grid=(batch, per_seq),
        in_specs=[
            q_spec,
            pl.BlockSpec((seq, WIDTH), lambda b, i: (b, 0)),
            pl.BlockSpec((seq // TK, WIDTH, TK), lambda b, i: (b, 0, 0)),
            q_spec,
            pl.BlockSpec((seq, LANES), lambda b, i: (b, 0)),
            pl.BlockSpec((TQ, LANES), lambda b, i: (b * per_seq + i, 0)),
            _resident((1, WIDTH)),
        ],
        out_specs=q_spec,
        out_shape=jax.ShapeDtypeStruct(dq.shape, BF16),
        scratch_shapes=_attn_scratch() + [
            pltpu.VMEM((N_IDX_HEADS, LANES, TQ), BF16),
            pltpu.VMEM((N_IDX_HEADS, TQ), F32),
            pltpu.VMEM((per_seq, TK, TQ), jnp.int32),
            pltpu.VMEM((per_seq, TK, TQ), jnp.int16),
        ],
        compiler_params=pltpu.CompilerParams(
            dimension_semantics=("parallel", "parallel"), vmem_limit_bytes=VMEM_LIMIT),
        name="dsa",
    )(dq, dk, dvt, qi, kib, kw, g)


def _ffn_kernel(x_ref, om_ref, od_ref, mod_ref, g_post_ref, g_pre_ref, g_postf_ref,
                wo_ref, wa_ref, wl_ref, wc_ref, bc_ref, wd_ref, o_ref,
                apad_sc, gated_sc, tail_sc, *, tiles_per_seq):
    tm = x_ref.shape[0]
    first = (pl.program_id(0) % tiles_per_seq) == 0
    mod = mod_ref[...]
    o = jnp.concatenate([om_ref[...], od_ref[...]], axis=1)
    y = jnp.dot(o, wo_ref[...], preferred_element_type=F32)
    x1 = x_ref[...] + mod[2:3] * _rms(y, g_post_ref[...])
    hb = (_rms(x1, g_pre_ref[...]) * (1.0 + mod[4:5]) + mod[3:4]).astype(BF16)
    for c in range(D_FF // FF_CHUNK):
        cols = slice(c * FF_CHUNK, (c + 1) * FF_CHUNK)
        a = jnp.dot(hb, wa_ref[:, cols], preferred_element_type=F32)
        apad_sc[0:SUBLANES, :] = jnp.where(first, 0.0, tail_sc[:, cols])
        apad_sc[SUBLANES:, :] = a
        tail_sc[:, cols] = a[tm - SUBLANES:, :]
        wc = wc_ref[:, cols]
        conv = (wc[0:1] * apad_sc[SUBLANES - 2:SUBLANES - 2 + tm, :]
                + wc[1:2] * apad_sc[SUBLANES - 1:SUBLANES - 1 + tm, :]
                + wc[2:3] * a + bc_ref[:, cols])
        u = jnp.dot(hb, wl_ref[:, cols], preferred_element_type=F32)
        gated_sc[:, cols] = (jax.nn.gelu(conv) * u).astype(BF16)
    y2 = jnp.dot(gated_sc[...], wd_ref[...], preferred_element_type=F32)
    o_ref[...] = x1 + mod[5:6] * _rms(y2, g_postf_ref[...])


def _ffn_call(x2, om, od, mod, g_post, g_pre, g_postf, wo, wa, wl, wc, bc, wd, batch, seq):
    m = x2.shape[0]
    tm = TM_FFN
    per_seq = seq // tm
    row = lambda i: (i, 0)
    return pl.pallas_call(
        functools.partial(_ffn_kernel, tiles_per_seq=per_seq),
        grid=(m // tm,),
        in_specs=[
            pl.BlockSpec((tm, D_MODEL), row),
            pl.BlockSpec((tm, WIDTH), row),
            pl.BlockSpec((tm, WIDTH), row),
            pl.BlockSpec((None, SUBLANES, D_MODEL), lambda i: (i // per_seq, 0, 0)),
            _resident((1, D_MODEL)), _resident((1, D_MODEL)), _resident((1, D_MODEL)),
            _resident((D_MODEL, D_MODEL)),
            _resident((D_MODEL, D_FF)), _resident((D_MODEL, D_FF)),
            _resident((CONV_WIDTH, D_FF)), _resident((1, D_FF)),
            _resident((D_FF, D_MODEL)),
        ],
        out_specs=pl.BlockSpec((tm, D_MODEL), row),
        out_shape=jax.ShapeDtypeStruct(x2.shape, F32),
        scratch_shapes=[
            pltpu.VMEM((tm + SUBLANES, FF_CHUNK), F32),
            pltpu.VMEM((tm, D_FF), BF16),
            pltpu.VMEM((SUBLANES, D_FF), F32),
        ],
        compiler_params=pltpu.CompilerParams(
            dimension_semantics=("arbitrary",), vmem_limit_bytes=VMEM_LIMIT),
        name="ffn",
    )(x2, om, od, mod, g_post, g_pre, g_postf, wo, wa, wl, wc, bc, wd)


def kernel(x, c, w_ada, b_ada, g_pre_mix, w_in, g_moba_out, g_dsa_out, w_out, g_post_mix,
           g_pre_ffn, w_up_act, w_up_lin, w_conv, b_conv, w_down, g_post_ffn):
    batch, seq, d = x.shape
    depth = w_ada.shape[0]
    n_blk = seq // MOBA_BLOCK
    assert d == D_MODEL and w_in.shape[-1] == PROJ_WIDTH
    assert seq % TM_PROJ == 0 and seq % TM_FFN == 0 and seq % (2 * TK) == 0 and TQ == TK
    assert min(DSA_TOPK, seq // 4) == DSA_TOPK and seq < 2 ** 13

    mod = _ada_call(c, w_ada, b_ada).reshape(depth, batch, N_MOD, D_MODEL)
    mod = jnp.pad(mod, ((0, 0), (0, 0), (0, SUBLANES - N_MOD), (0, 0)))
    tab = _rope_tables(seq)
    vec = lambda g: g.reshape(1, -1)

    x2 = x.reshape(batch * seq, D_MODEL)
    for l in range(depth):
        w_in_b = jnp.pad(w_in[l].astype(BF16), ((0, 0), (0, PROJ_PAD - PROJ_WIDTH)))
        wvt = jnp.stack([w_in[l][:, 2 * WIDTH:3 * WIDTH].T, w_in[l][:, 5 * WIDTH:6 * WIDTH].T]).astype(BF16)
        mq, mk, mvt, dq, dk, dvt, qi, kib, kw, km = _proj_call(
            x2, mod[l], vec(g_pre_mix[l]), w_in_b, wvt, tab, batch, seq)
        km = km.reshape(batch, n_blk, WIDTH)
        km = jnp.pad(km, ((0, 0), (0, -n_blk % BF16_ROWS), (0, 0)))
        o_moba = _moba_call(mq, mk, mvt, km, vec(g_moba_out[l]), batch, seq)
        o_dsa = _dsa_call(dq, dk, dvt, qi, kib, kw, vec(g_dsa_out[l]), batch, seq)
        x2 = _ffn_call(
            x2, o_moba, o_dsa, mod[l], vec(g_post_mix[l]), vec(g_pre_ffn[l]), vec(g_post_ffn[l]),
            w_out[l].astype(BF16), w_up_act[l].astype(BF16), w_up_lin[l].astype(BF16),
            w_conv[l], vec(b_conv[l]), w_down[l].astype(BF16), batch, seq)
    return x2.reshape(batch, seq, D_MODEL)
```
